```python
import jax, jax.numpy as jnp
from jax import lax
import numpy as np

D_MODEL = 1024
BATCH = 2
SEQ = 8192
DEPTH = 1

GRID_W = 64
CTX_LEN = 256
N_HEADS = 8
N_KV_HEADS = 2
HEAD_DIM = 128
Q_PER_KV = N_HEADS // N_KV_HEADS
ATTN_WIDTH = N_HEADS * HEAD_DIM
KV_WIDTH = N_KV_HEADS * HEAD_DIM
WINDOW = 128
BLOCK = 128
ROPE_THETA = 10000.0
D_RNN = D_MODEL
N_LRU_BLOCKS = 16
LRU_BLOCK = D_RNN // N_LRU_BLOCKS
LRU_C = 8.0
LRU_CONV_W = 4
LRU_CONV_LEFT = 2
N_DIRS = 2
D_FF = 3 * D_MODEL
FFN_CONV_W = 3
N_MOD = 6
EPS = 1e-6
NEG_INF = -1e30
IN_SPLITS = (D_RNN, D_RNN, ATTN_WIDTH, KV_WIDTH, KV_WIDTH, D_MODEL, D_MODEL)
D_IN = sum(IN_SPLITS)

kernel_name = 'hybrid_rglru_swa_convffn_prefix_dit'


def rms_norm(x, w):
    xf = x.astype(jnp.float32)
    y = xf * lax.rsqrt(jnp.mean(xf * xf, axis=-1, keepdims=True) + EPS)
    return (y * w.astype(jnp.float32)).astype(x.dtype)


def modulate(h, shift, scale):
    return h * (1.0 + scale) + shift


def dwconv(x, w, b, left):
    k_w, ch = w.shape
    y = lax.conv_general_dilated(x, w[:, None, :], window_strides=(1,), padding=[(left, k_w - 1 - left)],
                                 dimension_numbers=('NWC', 'WIO', 'NWC'), feature_group_count=ch)
    return y + b


def split_cols(p):
    idx = np.cumsum(IN_SPLITS)[:-1].tolist()
    return jnp.split(p, idx, axis=-1)


def linear_scan(a, b, h0):
    b = b.at[:, 0].add(a[:, 0] * h0)
    def combine(l, r):
        return (l[0] * r[0], r[0] * l[1] + r[1])
    return lax.associative_scan(combine, (a, b), axis=1)[1]


def rglru_dir(xc, w_a, b_a, w_x, b_x, lam, h0):
    bsz, length, ch = xc.shape
    xf = xc.astype(jnp.float32)
    xg = xf.reshape(bsz, length, N_LRU_BLOCKS, LRU_BLOCK)
    r = jax.nn.sigmoid(jnp.einsum('blnc,ncd->blnd', xg, w_a.astype(jnp.float32)).reshape(bsz, length, ch)
                       + b_a.astype(jnp.float32))
    i = jax.nn.sigmoid(jnp.einsum('blnc,ncd->blnd', xg, w_x.astype(jnp.float32)).reshape(bsz, length, ch)
                       + b_x.astype(jnp.float32))
    log_a = -LRU_C * r * jax.nn.softplus(-lam.astype(jnp.float32))
    a = jnp.exp(log_a)
    b = jnp.sqrt(-jnp.expm1(2.0 * log_a)) * (i * xf)
    return linear_scan(a, b, h0)


def rglru_states(xr, conv_w, conv_b, w_a, b_a, w_x, b_x, lam, h0_f, h0_b):
    xc = dwconv(xr, conv_w, conv_b, LRU_CONV_LEFT)
    h_f = rglru_dir(xc, w_a[0], b_a[0], w_x[0], b_x[0], lam[0], h0_f)
    h_b = jnp.flip(rglru_dir(jnp.flip(xc, axis=1), w_a[1], b_a[1], w_x[1], b_x[1], lam[1], h0_b), axis=1)
    return h_f, h_b


def rglru_output(h_f, h_b, gate, dtype):
    return ((h_f + h_b) * jax.nn.gelu(gate.astype(jnp.float32))).astype(dtype)


def axial_rope_tables(length):
    rows = length // GRID_W
    row = jnp.repeat(jnp.arange(rows), GRID_W)
    col = jnp.tile(jnp.arange(GRID_W), rows)
    half = HEAD_DIM // 2
    inv = ROPE_THETA ** (-jnp.arange(0, half, 2, dtype=jnp.float32) / half)
    ang = jnp.stack([row[:, None] * inv, col[:, None] * inv], axis=1)
    return jnp.cos(ang), jnp.sin(ang)


def apply_rope(x, cos, sin):
    bsz, length, heads, _ = x.shape
    xr = x.astype(jnp.float32).reshape(bsz, length, heads, 2, 2, HEAD_DIM // 4)
    x1, x2 = xr[..., 0, :], xr[..., 1, :]
    cs, sn = cos[None, :, None], sin[None, :, None]
    out = jnp.stack([x1 * cs - x2 * sn, x2 * cs + x1 * sn], axis=-2)
    return out.reshape(bsz, length, heads, HEAD_DIM)


def windowed_attention(q, k, v, kc, vc, sink):
    bsz, length = q.shape[:2]
    nb = length // BLOCK
    qb = (q * HEAD_DIM ** -0.5).reshape(bsz, nb, BLOCK, N_KV_HEADS, Q_PER_KV, HEAD_DIM)

    def neighbours(t):
        tb = t.reshape(bsz, nb, BLOCK, N_KV_HEADS, HEAD_DIM)
        tp = jnp.pad(tb, ((0, 0), (1, 1), (0, 0), (0, 0), (0, 0)))
        return jnp.concatenate([tp[:, :-2], tp[:, 1:-1], tp[:, 2:]], axis=2)

    kw, vw = neighbours(k), neighbours(v)
    s_loc = jnp.einsum('bnqkgd,bnjkd->bnkgqj', qb, kw)
    blk = jnp.arange(nb)[:, None, None]
    qpos = blk * BLOCK + jnp.arange(BLOCK)[None, :, None]
    kpos = (blk - 1) * BLOCK + jnp.arange(3 * BLOCK)[None, None, :]
    valid = (jnp.abs(kpos - qpos) <= WINDOW) & (kpos >= 0) & (kpos < length)
    s_loc = jnp.where(valid[None, :, None, None], s_loc, NEG_INF)
    s_ctx = jnp.einsum('bnqkgd,bckd->bnkgqc', qb, kc)
    sink_l = sink.astype(jnp.float32).reshape(1, 1, N_KV_HEADS, Q_PER_KV, 1)
    m = jnp.maximum(jnp.maximum(s_loc.max(-1), s_ctx.max(-1)), sink_l)
    p_loc = jnp.exp(s_loc - m[..., None])
    p_ctx = jnp.exp(s_ctx - m[..., None])
    denom = p_loc.sum(-1) + p_ctx.sum(-1) + jnp.exp(sink_l - m)
    o = jnp.einsum('bnkgqj,bnjkd->bnkgqd', p_loc, vw) + jnp.einsum('bnkgqc,bckd->bnkgqd', p_ctx, vc)
    o = o / denom[..., None]
    return o.transpose(0, 1, 4, 2, 3, 5).reshape(bsz, length, ATTN_WIDTH)


def context_attention(qc, kc, vc, sink):
    bsz, clen = qc.shape[:2]
    q = (qc * HEAD_DIM ** -0.5).reshape(bsz, clen, N_KV_HEADS, Q_PER_KV, HEAD_DIM)
    s = jnp.einsum('bqkgd,bckd->bkgqc', q, kc)
    sink_l = sink.astype(jnp.float32).reshape(1, N_KV_HEADS, Q_PER_KV, 1)
    m = jnp.maximum(s.max(-1), sink_l)
    p = jnp.exp(s - m[..., None])
    denom = p.sum(-1) + jnp.exp(sink_l - m)
    o = jnp.einsum('bkgqc,bckd->bkgqd', p, vc) / denom[..., None]
    return o.transpose(0, 3, 1, 2, 4).reshape(bsz, clen, ATTN_WIDTH)


def merge_branches(y_rnn, y_attn, g_rnn, g_attn, w_o_rnn, w_o_attn, w_out):
    y = jax.nn.sigmoid(g_rnn) * (y_rnn @ w_o_rnn) + jax.nn.sigmoid(g_attn) * (y_attn @ w_o_attn)
    return y @ w_out


def conv_ffn(h, w_up, conv_w, conv_b, w_down):
    u = dwconv(h @ w_up, conv_w, conv_b, FFN_CONV_W // 2)
    a, v = jnp.split(u, 2, axis=-1)
    return (jax.nn.gelu(a) * v) @ w_down


def setup_inputs(seed: int = 0) -> dict:
    key = jax.random.key(seed)
    ks = jax.random.split(key, 32)

    def nrm(k, shape, scale):
        return jax.random.normal(k, shape, jnp.float32) * scale

    u = jax.random.uniform(ks[14], (DEPTH, N_DIRS, D_RNN), jnp.float32, 0.9, 0.999)
    a0 = u ** (1.0 / LRU_C)
    return {
        'x': nrm(ks[0], (BATCH, SEQ, D_MODEL), 1.0),
        'c': nrm(ks[1], (BATCH, D_MODEL), 1.0),
        'ctx': nrm(ks[2], (BATCH, CTX_LEN, D_MODEL), 1.0),
        'c_ctx': nrm(ks[3], (D_MODEL,), 1.0),
        'w_mod': nrm(ks[4], (DEPTH, D_MODEL, N_MOD * D_MODEL), 0.5 * D_MODEL ** -0.5),
        'b_mod': nrm(ks[5], (DEPTH, N_MOD * D_MODEL), 0.02),
        'norm1_w': 1.0 + nrm(ks[6], (DEPTH, D_MODEL), 0.02),
        'w_in': nrm(ks[7], (DEPTH, D_MODEL, D_IN), D_MODEL ** -0.5),
        'lru_conv_w': nrm(ks[8], (DEPTH, LRU_CONV_W, D_RNN), LRU_CONV_W ** -0.5),
        'lru_conv_b': nrm(ks[9], (DEPTH, D_RNN), 0.02),
        'gate_a_w': nrm(ks[10], (DEPTH, N_DIRS, N_LRU_BLOCKS, LRU_BLOCK, LRU_BLOCK), LRU_BLOCK ** -0.5),
        'gate_a_b': nrm(ks[11], (DEPTH, N_DIRS, D_RNN), 0.02),
        'gate_x_w': nrm(ks[12], (DEPTH, N_DIRS, N_LRU_BLOCKS, LRU_BLOCK, LRU_BLOCK), LRU_BLOCK ** -0.5),
        'gate_x_b': nrm(ks[13], (DEPTH, N_DIRS, D_RNN), 0.02),
        'lru_lambda': jnp.log(a0) - jnp.log1p(-a0),
        'sink_logit': nrm(ks[15], (DEPTH, N_HEADS), 0.5),
        'w_o_rnn': nrm(ks[16], (DEPTH, D_RNN, D_MODEL), D_RNN ** -0.5),
        'w_o_attn': nrm(ks[17], (DEPTH, ATTN_WIDTH, D_MODEL), ATTN_WIDTH ** -0.5),
        'w_out': nrm(ks[18], (DEPTH, D_MODEL, D_MODEL), D_MODEL ** -0.5),
        'norm2_w': 1.0 + nrm(ks[19], (DEPTH, D_MODEL), 0.02),
        'w_up': nrm(ks[20], (DEPTH, D_MODEL, 2 * D_FF), D_MODEL ** -0.5),
        'ffn_conv_w': nrm(ks[21], (DEPTH, FFN_CONV_W, 2 * D_FF), FFN_CONV_W ** -0.5),
        'ffn_conv_b': nrm(ks[22], (DEPTH, 2 * D_FF), 0.02),
        'w_down': nrm(ks[23], (DEPTH, D_FF, D_MODEL), D_FF ** -0.5),
        'final_norm_w': 1.0 + nrm(ks[24], (D_MODEL,), 0.02),
    }


def reference(x, c, ctx, c_ctx, w_mod, b_mod, norm1_w, w_in, lru_conv_w, lru_conv_b, gate_a_w, gate_a_b,
              gate_x_w, gate_x_b, lru_lambda, sink_logit, w_o_rnn, w_o_attn, w_out, norm2_w, w_up,
              ffn_conv_w, ffn_conv_b, w_down, final_norm_w):
    bsz, length, _ = x.shape
    cos, sin = axial_rope_tables(length)
    silu_c = jax.nn.silu(c)
    silu_cc = jax.nn.silu(c_ctx)
    for l in range(DEPTH):
        last = l == DEPTH - 1
        mod_x = (silu_c @ w_mod[l] + b_mod[l])[:, None, :]
        mod_c = (silu_cc @ w_mod[l] + b_mod[l])[None, None, :]
        sh1, sc1, g1, sh2, sc2, g2 = jnp.split(mod_x, N_MOD, axis=-1)
        csh1, csc1, cg1, csh2, csc2, cg2 = jnp.split(mod_c, N_MOD, axis=-1)

        hx = modulate(rms_norm(x, norm1_w[l]), sh1, sc1)
        hc = modulate(rms_norm(ctx, norm1_w[l]), csh1, csc1)
        rx, rgx, qx, kx, vx, gax, gbx = split_cols(hx @ w_in[l])
        rc, rgc, qc, kc, vc, gac, gbc = split_cols(hc @ w_in[l])

        lru = (lru_conv_w[l], lru_conv_b[l], gate_a_w[l], gate_a_b[l], gate_x_w[l], gate_x_b[l], lru_lambda[l])
        zeros = jnp.zeros((bsz, D_RNN), jnp.float32)
        hf_c, hb_c = rglru_states(rc, *lru, zeros, zeros)
        hf_x, hb_x = rglru_states(rx, *lru, hf_c[:, -1], hb_c[:, 0])
        y_rx = rglru_output(hf_x, hb_x, rgx, x.dtype)

        kc_h = kc.reshape(bsz, CTX_LEN, N_KV_HEADS, HEAD_DIM).astype(jnp.float32)
        vc_h = vc.reshape(bsz, CTX_LEN, N_KV_HEADS, HEAD_DIM).astype(jnp.float32)
        q_lat = apply_rope(qx.reshape(bsz, length, N_HEADS, HEAD_DIM), cos, sin)
        k_lat = apply_rope(kx.reshape(bsz, length, N_KV_HEADS, HEAD_DIM), cos, sin)
        v_lat = vx.reshape(bsz, length, N_KV_HEADS, HEAD_DIM).astype(jnp.float32)
        y_ax = windowed_attention(q_lat, k_lat, v_lat, kc_h, vc_h, sink_logit[l]).astype(x.dtype)

        x = x + g1 * merge_branches(y_rx, y_ax, gax, gbx, w_o_rnn[l], w_o_attn[l], w_out[l])
        hx2 = modulate(rms_norm(x, norm2_w[l]), sh2, sc2)
        x = x + g2 * conv_ffn(hx2, w_up[l], ffn_conv_w[l], ffn_conv_b[l], w_down[l])

        if not last:
            y_rc = rglru_output(hf_c, hb_c, rgc, ctx.dtype)
            qc_h = qc.reshape(bsz, CTX_LEN, N_HEADS, HEAD_DIM).astype(jnp.float32)
            y_ac = context_attention(qc_h, kc_h, vc_h, sink_logit[l]).astype(ctx.dtype)
            ctx = ctx + cg1 * merge_branches(y_rc, y_ac, gac, gbc, w_o_rnn[l], w_o_attn[l], w_out[l])
            hc2 = modulate(rms_norm(ctx, norm2_w[l]), csh2, csc2)
            ctx = ctx + cg2 * conv_ffn(hc2, w_up[l], ffn_conv_w[l], ffn_conv_b[l], w_down[l])
    return rms_norm(x, final_norm_w)
```

```python
import functools

import numpy as np
import jax
import jax.numpy as jnp
from jax import lax
from jax.experimental import pallas as pl
from jax.experimental.pallas import tpu as pltpu

F32 = jnp.float32
BF16 = jnp.bfloat16

D_MODEL = 1024
N_HEADS = 8
N_KV_HEADS = 2
HEAD_DIM = 128
Q_PER_KV = N_HEADS // N_KV_HEADS
ATTN_WIDTH = N_HEADS * HEAD_DIM
KV_WIDTH = N_KV_HEADS * HEAD_DIM
WINDOW = 128
BLOCK = 128
GRID_W = 64
ROPE_THETA = 10000.0
D_RNN = D_MODEL
N_LRU_BLOCKS = 16
LRU_BLOCK = D_RNN // N_LRU_BLOCKS
LRU_C = 8.0
LRU_CONV_W = 4
LRU_CONV_LEFT = 2
D_FF = 3 * D_MODEL
FFN_CONV_W = 3
N_MOD = 6
EPS = 1e-6
NEG_INF = -1e30

COL_R = 0
COL_RG = COL_R + D_RNN
COL_Q = COL_RG + D_RNN
COL_K = COL_Q + ATTN_WIDTH
COL_V = COL_K + KV_WIDTH
COL_GA = COL_V + KV_WIDTH
COL_GB = COL_GA + D_MODEL
D_IN = COL_GB + D_MODEL

SUBLANES = 8
BF16_SUBLANES = 16
LANES = 128
MXU_DIM = 256
VMEM_LIMIT = 56 * 1024 * 1024

GATE_TILE = MXU_DIM
GATE_TILES = D_RNN // GATE_TILE
FFN_CHUNK = 512


def _sigmoid(v):
    return 1.0 / (1.0 + jnp.exp(-v))


def _gelu(v):
    return 0.5 * v * (1.0 + jnp.tanh(np.sqrt(2.0 / np.pi).astype(np.float32) * (v + 0.044715 * (v * v * v))))


def _const_spec(shape):
    nd = len(shape)
    return pl.BlockSpec(shape, lambda *_: (0,) * nd, pipeline_mode=pl.Buffered(1))


def _mod_kernel(c_ref, w_ref, b_ref, o_ref):
    c = c_ref[...]
    s = c * _sigmoid(c)
    o_ref[...] = jnp.dot(s, w_ref[...], preferred_element_type=F32) + b_ref[...]


def _mod_call(cs, w_mod, b_mod):
    n_out = w_mod.shape[1]
    tn = n_out // 4
    return pl.pallas_call(
        _mod_kernel,
        grid=(n_out // tn,),
        in_specs=[pl.BlockSpec((SUBLANES, D_MODEL), lambda j: (0, 0)),
                  pl.BlockSpec((D_MODEL, tn), lambda j: (0, j)),
                  pl.BlockSpec((1, tn), lambda j: (0, j))],
        out_specs=pl.BlockSpec((SUBLANES, tn), lambda j: (0, j)),
        out_shape=jax.ShapeDtypeStruct((SUBLANES, n_out), F32),
        compiler_params=pltpu.CompilerParams(dimension_semantics=("arbitrary",),
                                             vmem_limit_bytes=VMEM_LIMIT),
        name="mod",
    )(cs, w_mod, b_mod)


def _rope(v, cos, sin):
    return v * cos + pltpu.roll(v, HEAD_DIM // 2, 1) * sin


def _inproj_kernel(*refs, plan, rope):
    x_ref, mod_ref, nw_ref, w_ref = refs[:4]
    n_in = 8 if rope else 4
    out_refs = refs[n_in:n_in + len(plan)]
    h_scr = refs[-1]
    x = x_ref[0]
    ms = jnp.mean(x * x, axis=-1, keepdims=True)
    y = x * lax.rsqrt(ms + EPS) * nw_ref[...]
    h = y * (1.0 + mod_ref[0, 1:2, :]) + mod_ref[0, 0:1, :]
    h_scr[...] = h.astype(BF16)
    for (c0, width, kind), o_ref in zip(plan, out_refs):
        res = jnp.dot(h_scr[...], w_ref[:, c0:c0 + width], preferred_element_type=F32)
        if kind in ("rope_q", "rope_k"):
            cos_ref, sin_ref = (refs[4], refs[5]) if kind == "rope_q" else (refs[6], refs[7])
            cos, sin = cos_ref[...], sin_ref[...]
            for hh in range(width // HEAD_DIM):
                sl = slice(hh * HEAD_DIM, (hh + 1) * HEAD_DIM)
                o_ref[0, :, sl] = _rope(res[:, sl], cos, sin).astype(o_ref.dtype)
        else:
            o_ref[0] = res.astype(o_ref.dtype)


def _inproj_call(x, mod, mod_row0, nw, w_in, tables, plan, tm):
    bsz, length, _ = x.shape
    rope = tables is not None
    tok = lambda b, i: (b, i, 0)
    in_specs = [pl.BlockSpec((1, tm, D_MODEL), tok),
                pl.BlockSpec((1, N_MOD, D_MODEL), lambda b, i: (b * (mod_row0 == 0) + mod_row0, 0, 0)),
                _const_spec((1, D_MODEL)),
                _const_spec((D_MODEL, D_IN))]
    args = [x, mod, nw, w_in]
    if rope:
        in_specs += [pl.BlockSpec((tm, HEAD_DIM), lambda b, i: (i, 0))] * 4
        args += list(tables)
    out_specs, out_shape = [], []
    for (_, width, kind) in plan:
        dt = F32 if kind == "f32" else BF16
        out_specs.append(pl.BlockSpec((1, tm, width), tok))
        out_shape.append(jax.ShapeDtypeStruct((bsz, length, width), dt))
    return pl.pallas_call(
        functools.partial(_inproj_kernel, plan=tuple(plan), rope=rope),
        grid=(bsz, length // tm),
        in_specs=in_specs, out_specs=out_specs, out_shape=out_shape,
        scratch_shapes=[pltpu.VMEM((tm, D_MODEL), BF16)],
        compiler_params=pltpu.CompilerParams(dimension_semantics=("parallel", "parallel"),
                                             vmem_limit_bytes=VMEM_LIMIT),
        name="inproj_x" if rope else "inproj_ctx",
    )(*args)


HALO = SUBLANES


def _scan_kernel(r_ref, rp_ref, rn_ref, rc_ref, cw_ref, cb_ref, wg_ref, bg_ref, lam_ref,
                 h_ref, rbuf, a_scr, b_scr, hc_scr, *, tm, nb, clen):
    d = pl.program_id(1)
    i = pl.program_id(2)
    blk = i + d * (nb - 1 - 2 * i)

    lam = lam_ref[0]
    neg = -lam
    softplus = jnp.maximum(neg, 0.0) + jnp.log(1.0 + jnp.exp(-jnp.abs(neg)))
    decay = -LRU_C * softplus

    def conv_gates(n):
        xc = cb_ref[...] + sum(cw_ref[k:k + 1, :] * rbuf[pl.ds(HALO - LRU_CONV_LEFT + k, n), :]
                               for k in range(LRU_CONV_W))
        for kt in range(GATE_TILES):
            cs = slice(kt * GATE_TILE, (kt + 1) * GATE_TILE)
            xk = xc[:, cs]
            g = jnp.dot(xk.astype(BF16), wg_ref[0, kt], preferred_element_type=F32)
            rg = _sigmoid(g[:, :GATE_TILE] + bg_ref[0, 0:1, cs])
            ig = _sigmoid(g[:, GATE_TILE:] + bg_ref[0, 1:2, cs])
            log_a = decay[:, cs] * rg
            a = jnp.exp(log_a)
            a_scr[pl.ds(0, n), cs] = a
            b_scr[pl.ds(0, n), cs] = jnp.sqrt(1.0 - a * a) * (ig * xk)

    def run_scan(n, h0, out):
        def step(t, h):
            row = t + d * (n - 1 - 2 * t)
            h = a_scr[pl.ds(row, 1), :] * h + b_scr[pl.ds(row, 1), :]
            if out:
                h_ref[0, 0, pl.ds(row, 1), :] = h
            return h
        return lax.fori_loop(0, n, step, h0, unroll=8)

    @pl.when(i == 0)
    def _():
        zeros = jnp.zeros((HALO, D_RNN), F32)
        rbuf[pl.ds(0, HALO), :] = zeros
        rbuf[pl.ds(HALO, clen), :] = rc_ref[0]
        rbuf[pl.ds(HALO + clen, HALO), :] = zeros
        conv_gates(clen)
        hc_scr[...] = run_scan(clen, jnp.zeros((1, D_RNN), F32), False)

    rbuf[pl.ds(0, HALO), :] = jnp.where(blk > 0, rp_ref[0], 0.0)
    rbuf[pl.ds(HALO, tm), :] = r_ref[0]
    rbuf[pl.ds(HALO + tm, HALO), :] = jnp.where(blk < nb - 1, rn_ref[0], 0.0)
    conv_gates(tm)
    hc_scr[...] = run_scan(tm, hc_scr[...], True)


def _scan_call(r, rc, conv_w, conv_b, wg, bg, lam, tm):
    bsz, length, ch = r.shape
    clen = rc.shape[1]
    nb = length // tm
    hb = tm // HALO
    n_hb = length // HALO
    blk = lambda d, i: i + d * (nb - 1 - 2 * i)
    return pl.pallas_call(
        functools.partial(_scan_kernel, tm=tm, nb=nb, clen=clen),
        grid=(bsz, 2, nb),
        in_specs=[pl.BlockSpec((1, tm, ch), lambda b, d, i: (b, blk(d, i), 0)),
                  pl.BlockSpec((1, HALO, ch), lambda b, d, i: (b, jnp.maximum(blk(d, i) * hb - 1, 0), 0)),
                  pl.BlockSpec((1, HALO, ch), lambda b, d, i: (b, jnp.minimum((blk(d, i) + 1) * hb, n_hb - 1), 0)),
                  pl.BlockSpec((1, clen, ch), lambda b, d, i: (b, 0, 0)),
                  _const_spec((LRU_CONV_W, ch)),
                  _const_spec((1, ch)),
                  pl.BlockSpec((1, GATE_TILES, GATE_TILE, 2 * GATE_TILE), lambda b, d, i: (d, 0, 0, 0)),
                  pl.BlockSpec((1, 2, ch), lambda b, d, i: (d, 0, 0)),
                  pl.BlockSpec((1, 1, ch), lambda b, d, i: (d, 0, 0))],
        out_specs=pl.BlockSpec((1, 1, tm, ch), lambda b, d, i: (d, b, blk(d, i), 0)),
        out_shape=jax.ShapeDtypeStruct((2, bsz, length, ch), F32),
        scratch_shapes=[pltpu.VMEM((max(tm, clen) + 2 * HALO, ch), F32),
                        pltpu.VMEM((max(tm, clen), ch), F32),
                        pltpu.VMEM((max(tm, clen), ch), F32),
                        pltpu.VMEM((1, ch), F32)],
        compiler_params=pltpu.CompilerParams(dimension_semantics=("parallel", "parallel", "arbitrary"),
                                             vmem_limit_bytes=VMEM_LIMIT),
        name="scan",
    )(r, r, r, rc, conv_w, conv_b, wg, bg, lam)


def _attn_kernel(sink_ref, q_ref, k_ref, kp_ref, kn_ref, v_ref, vp_ref, vn_ref, kc_ref, vc_ref,
                 o_ref, kcat, vcat, *, nqb, length):
    i = pl.program_id(1)
    tq = nqb * BLOCK
    kcat[pl.ds(0, BLOCK), :] = kp_ref[0]
    kcat[pl.ds(BLOCK, tq), :] = k_ref[0]
    kcat[pl.ds(BLOCK + tq, BLOCK), :] = kn_ref[0]
    vcat[pl.ds(0, BLOCK), :] = vp_ref[0]
    vcat[pl.ds(BLOCK, tq), :] = v_ref[0]
    vcat[pl.ds(BLOCK + tq, BLOCK), :] = vn_ref[0]

    rows = Q_PER_KV * BLOCK
    qi = lax.broadcasted_iota(jnp.int32, (rows, 3 * BLOCK), 0) % BLOCK
    kj = lax.broadcasted_iota(jnp.int32, (rows, 3 * BLOCK), 1)
    rel = kj - qi
    band = (rel >= BLOCK - WINDOW) & (rel <= BLOCK + WINDOW)
    nt = (((1,), (1,)), ((), ()))
    for jq in range(nqb):
        kpos = (i * nqb + jq - 1) * BLOCK + kj
        mask = band & (kpos >= 0) & (kpos < length)
        for g in range(N_KV_HEADS):
            gs = slice(g * HEAD_DIM, (g + 1) * HEAD_DIM)
            heads = [g * Q_PER_KV + hh for hh in range(Q_PER_KV)]
            q4 = jnp.concatenate(
                [q_ref[0, pl.ds(jq * BLOCK, BLOCK), h * HEAD_DIM:(h + 1) * HEAD_DIM] for h in heads], axis=0)
            sink = jnp.concatenate([jnp.full((BLOCK, 1), sink_ref[h], F32) for h in heads], axis=0)
            s_loc = lax.dot_general(q4, kcat[pl.ds(jq * BLOCK, 3 * BLOCK), gs], nt, preferred_element_type=F32)
            s_loc = jnp.where(mask, s_loc, NEG_INF)
            s_ctx = lax.dot_general(q4, kc_ref[0, :, gs], nt, preferred_element_type=F32)
            m = jnp.maximum(jnp.maximum(jnp.max(s_loc, axis=-1, keepdims=True),
                                        jnp.max(s_ctx, axis=-1, keepdims=True)), sink)
            p_loc = jnp.exp(s_loc - m)
            p_ctx = jnp.exp(s_ctx - m)
            denom = (jnp.sum(p_loc, axis=-1, keepdims=True) + jnp.sum(p_ctx, axis=-1, keepdims=True)
                     + jnp.exp(sink - m))
            o = (jnp.dot(p_loc.astype(BF16), vcat[pl.ds(jq * BLOCK, 3 * BLOCK), gs], preferred_element_type=F32)
                 + jnp.dot(p_ctx.astype(BF16), vc_ref[0, :, gs], preferred_element_type=F32))
            o = o * (1.0 / denom)
            for hh, h in enumerate(heads):
                o_ref[0, pl.ds(jq * BLOCK, BLOCK), h * HEAD_DIM:(h + 1) * HEAD_DIM] = (
                    o[hh * BLOCK:(hh + 1) * BLOCK].astype(o_ref.dtype))


def _attn_call(sink, q, k, v, kc, vc, nqb):
    bsz, length, _ = q.shape
    clen = kc.shape[1]
    tq = nqb * BLOCK
    n_blk = length // BLOCK
    prev = lambda b, i: (b, jnp.maximum(i * nqb - 1, 0), 0)
    nxt = lambda b, i: (b, jnp.minimum((i + 1) * nqb, n_blk - 1), 0)
    main = lambda b, i: (b, i, 0)
    kv_specs = [pl.BlockSpec((1, tq, KV_WIDTH), main),
                pl.BlockSpec((1, BLOCK, KV_WIDTH), prev),
                pl.BlockSpec((1, BLOCK, KV_WIDTH), nxt)]
    ctx_spec = pl.BlockSpec((1, clen, KV_WIDTH), lambda b, i: (b, 0, 0))
    return pl.pallas_call(
        functools.partial(_attn_kernel, nqb=nqb, length=length),
        grid=(bsz, length // tq),
        in_specs=[pl.BlockSpec(memory_space=pltpu.SMEM),
                  pl.BlockSpec((1, tq, ATTN_WIDTH), main)] + kv_specs + kv_specs + [ctx_spec, ctx_spec],
        out_specs=pl.BlockSpec((1, tq, ATTN_WIDTH), main),
        out_shape=jax.ShapeDtypeStruct((bsz, length, ATTN_WIDTH), BF16),
        scratch_shapes=[pltpu.VMEM((tq + 2 * BLOCK, KV_WIDTH), BF16),
                        pltpu.VMEM((tq + 2 * BLOCK, KV_WIDTH), BF16)],
        compiler_params=pltpu.CompilerParams(dimension_semantics=("parallel", "parallel"),
                                             vmem_limit_bytes=VMEM_LIMIT),
        name="attn",
    )(sink, q, k, k, k, v, v, v, kc, vc)


def _merge_kernel(hf_ref, hb_ref, rg_ref, ya_ref, ga_ref, gb_ref, x_ref, mod_ref,
                  wr_ref, wa_ref, wo_ref, nw_ref, x1_ref, h2_ref):
    y_rnn = (hf_ref[0, 0] + hb_ref[0, 0]) * _gelu(rg_ref[0])
    t_rnn = jnp.dot(y_rnn.astype(BF16), wr_ref[...], preferred_element_type=F32)
    t_att = jnp.dot(ya_ref[0], wa_ref[...], preferred_element_type=F32)
    y = _sigmoid(ga_ref[0]) * t_rnn + _sigmoid(gb_ref[0]) * t_att
    z = jnp.dot(y.astype(BF16), wo_ref[...], preferred_element_type=F32)
    x1 = x_ref[0] + mod_ref[0, 2:3, :] * z
    x1_ref[0] = x1
    ms = jnp.mean(x1 * x1, axis=-1, keepdims=True)
    n2 = x1 * lax.rsqrt(ms + EPS) * nw_ref[...]
    h2_ref[0] = (n2 * (1.0 + mod_ref[0, 4:5, :]) + mod_ref[0, 3:4, :]).astype(h2_ref.dtype)


def _merge_call(h, rg, ya, ga, gb, x, mod, w_o_rnn, w_o_attn, w_out, nw2, tm):
    bsz, length, _ = x.shape
    tok = lambda b, i: (b, i, 0)
    blk = pl.BlockSpec((1, tm, D_MODEL), tok)
    return pl.pallas_call(
        _merge_kernel,
        grid=(bsz, length // tm),
        in_specs=[pl.BlockSpec((1, 1, tm, D_RNN), lambda b, i: (0, b, i, 0)),
                  pl.BlockSpec((1, 1, tm, D_RNN), lambda b, i: (1, b, i, 0)),
                  blk, blk, blk, blk, blk,
                  pl.BlockSpec((1, N_MOD, D_MODEL), lambda b, i: (b, 0, 0)),
                  _const_spec((D_RNN, D_MODEL)), _const_spec((ATTN_WIDTH, D_MODEL)),
                  _const_spec((D_MODEL, D_MODEL)), _const_spec((1, D_MODEL))],
        out_specs=[blk, blk],
        out_shape=[jax.ShapeDtypeStruct((bsz, length, D_MODEL), F32),
                   jax.ShapeDtypeStruct((bsz, length, D_MODEL), BF16)],
        compiler_params=pltpu.CompilerParams(dimension_semantics=("parallel", "parallel"),
                                             vmem_limit_bytes=VMEM_LIMIT),
        name="merge",
    )(h, h, rg, ya, ga, gb, x, mod, w_o_rnn, w_o_attn, w_out, nw2)


FFN_HALO = BF16_SUBLANES


def _ffn_kernel(h_ref, hp_ref, hn_ref, x1_ref, mod_ref, wu_ref, cw_ref, cb_ref, wd_ref, fw_ref,
                o_ref, hbuf, ua_scr, uv_scr, acc, *, tm, nb):
    i = pl.program_id(1)
    hbuf[pl.ds(0, FFN_HALO), :] = jnp.where(i > 0, hp_ref[0], jnp.zeros_like(hp_ref[0]))
    hbuf[pl.ds(FFN_HALO, tm), :] = h_ref[0]
    hbuf[pl.ds(FFN_HALO + tm, FFN_HALO), :] = jnp.where(i < nb - 1, hn_ref[0], jnp.zeros_like(hn_ref[0]))

    def conv(u_scr, c0):
        cs = slice(c0, c0 + FFN_CHUNK)
        return cb_ref[:, cs] + sum(cw_ref[k:k + 1, cs] * u_scr[pl.ds(FFN_HALO - FFN_CONV_W // 2 + k, tm), :]
                                   for k in range(FFN_CONV_W))

    for c in range(D_FF // FFN_CHUNK):
        ca, cv = c * FFN_CHUNK, D_FF + c * FFN_CHUNK
        ua_scr[...] = jnp.dot(hbuf[...], wu_ref[:, ca:ca + FFN_CHUNK], preferred_element_type=F32)
        uv_scr[...] = jnp.dot(hbuf[...], wu_ref[:, cv:cv + FFN_CHUNK], preferred_element_type=F32)
        g = (_gelu(conv(ua_scr, ca)) * conv(uv_scr, cv)).astype(BF16)
        part = jnp.dot(g, wd_ref[ca:ca + FFN_CHUNK, :], preferred_element_type=F32)
        if c == 0:
            acc[...] = part
        else:
            acc[...] += part
    x2 = x1_ref[0] + mod_ref[0, 5:6, :] * acc[...]
    ms = jnp.mean(x2 * x2, axis=-1, keepdims=True)
    o_ref[0] = x2 * lax.rsqrt(ms + EPS) * fw_ref[...]


def _ffn_call(h2, x1, mod, w_up, conv_w, conv_b, w_down, fw, tm):
    bsz, length, _ = x1.shape
    nb = length // tm
    hb = tm // FFN_HALO
    n_hb = length // FFN_HALO
    tok = lambda b, i: (b, i, 0)
    return pl.pallas_call(
        functools.partial(_ffn_kernel, tm=tm, nb=nb),
        grid=(bsz, nb),
        in_specs=[pl.BlockSpec((1, tm, D_MODEL), tok),
                  pl.BlockSpec((1, FFN_HALO, D_MODEL), lambda b, i: (b, jnp.maximum(i * hb - 1, 0), 0)),
                  pl.BlockSpec((1, FFN_HALO, D_MODEL), lambda b, i: (b, jnp.minimum((i + 1) * hb, n_hb - 1), 0)),
                  pl.BlockSpec((1, tm, D_MODEL), tok),
                  pl.BlockSpec((1, N_MOD, D_MODEL), lambda b, i: (b, 0, 0)),
                  _const_spec((D_MODEL, 2 * D_FF)), _const_spec((FFN_CONV_W, 2 * D_FF)),
                  _const_spec((1, 2 * D_FF)), _const_spec((D_FF, D_MODEL)), _const_spec((1, D_MODEL))],
        out_specs=pl.BlockSpec((1, tm, D_MODEL), tok),
        out_shape=jax.ShapeDtypeStruct((bsz, length, D_MODEL), F32),
        scratch_shapes=[pltpu.VMEM((tm + 2 * FFN_HALO, D_MODEL), BF16),
                        pltpu.VMEM((tm + 2 * FFN_HALO, FFN_CHUNK), F32),
                        pltpu.VMEM((tm + 2 * FFN_HALO, FFN_CHUNK), F32),
                        pltpu.VMEM((tm, D_MODEL), F32)],
        compiler_params=pltpu.CompilerParams(dimension_semantics=("parallel", "arbitrary"),
                                             vmem_limit_bytes=VMEM_LIMIT),
        name="ffn",
    )(h2, h2, h2, x1, mod, w_up, conv_w, conv_b, w_down, fw)


def _head_perm():
    q4 = HEAD_DIM // 4
    return np.concatenate([np.arange(0, q4), np.arange(2 * q4, 3 * q4), np.arange(q4, 2 * q4),
                           np.arange(3 * q4, 4 * q4)])


def _w_in_columns():
    idx = np.arange(D_IN)
    perm = _head_perm()
    for h in range(N_HEADS):
        idx[COL_Q + h * HEAD_DIM:COL_Q + (h + 1) * HEAD_DIM] = COL_Q + h * HEAD_DIM + perm
    for h in range(N_KV_HEADS):
        idx[COL_K + h * HEAD_DIM:COL_K + (h + 1) * HEAD_DIM] = COL_K + h * HEAD_DIM + perm
    return idx


def _rope_tables(length):
    rows = length // GRID_W
    row = jnp.repeat(jnp.arange(rows), GRID_W)
    col = jnp.tile(jnp.arange(GRID_W), rows)
    half = HEAD_DIM // 2
    inv = ROPE_THETA ** (-jnp.arange(0, half, 2, dtype=F32) / half)
    ar, ac = row[:, None] * inv, col[:, None] * inv
    cos = jnp.concatenate([jnp.cos(ar), jnp.cos(ac), jnp.cos(ar), jnp.cos(ac)], axis=1)
    sin = jnp.concatenate([-jnp.sin(ar), -jnp.sin(ac), jnp.sin(ar), jnp.sin(ac)], axis=1)
    scale = HEAD_DIM ** -0.5
    return cos * scale, sin * scale, cos, sin


def _gate_weights(w_a, w_x):
    per = GATE_TILE // LRU_BLOCK
    eye = jnp.eye(per, dtype=F32)

    def bd(w):
        w = w.reshape(2, GATE_TILES, per, LRU_BLOCK, LRU_BLOCK)
        return jnp.einsum('dtpcn,pq->dtpcqn', w, eye).reshape(2, GATE_TILES, GATE_TILE, GATE_TILE)

    return jnp.concatenate([bd(w_a), bd(w_x)], axis=-1).astype(BF16)


def kernel(x, c, ctx, c_ctx, w_mod, b_mod, norm1_w, w_in, lru_conv_w, lru_conv_b, gate_a_w, gate_a_b,
           gate_x_w, gate_x_b, lru_lambda, sink_logit, w_o_rnn, w_o_attn, w_out, norm2_w, w_up,
           ffn_conv_w, ffn_conv_b, w_down, final_norm_w):
    bsz, length, _ = x.shape
    l = 0

    cs = jnp.concatenate([c, c_ctx[None], jnp.zeros((SUBLANES - bsz - 1, D_MODEL), F32)], axis=0)
    mod = _mod_call(cs, w_mod[l], b_mod[l][None]).reshape(SUBLANES, N_MOD, D_MODEL)

    w_in_b = w_in[l][:, _w_in_columns()].astype(BF16)
    tables = _rope_tables(length)
    nw1 = norm1_w[l][None]

    plan_x = [(COL_R, D_RNN, "f32"), (COL_RG, D_RNN, "f32"), (COL_Q, ATTN_WIDTH, "rope_q"),
              (COL_K, KV_WIDTH, "rope_k"), (COL_V, KV_WIDTH, "bf16"), (COL_GA, D_MODEL, "f32"),
              (COL_GB, D_MODEL, "f32")]
    r, rg, q, k, v, ga, gb = _inproj_call(x, mod, 0, nw1, w_in_b, tables, plan_x, tm=512)
    plan_c = [(COL_R, D_RNN, "f32"), (COL_K, KV_WIDTH, "bf16"), (COL_V, KV_WIDTH, "bf16")]
    rc, kc, vc = _inproj_call(ctx, mod, bsz, nw1, w_in_b, None, plan_c, tm=ctx.shape[1])

    wg = _gate_weights(gate_a_w[l], gate_x_w[l])
    bg = jnp.stack([gate_a_b[l], gate_x_b[l]], axis=1)
    h = _scan_call(r, rc, lru_conv_w[l], lru_conv_b[l][None], wg, bg, lru_lambda[l][:, None, :], tm=512)

    ya = _attn_call(sink_logit[l], q, k, v, kc, vc, nqb=4)

    x1, h2 = _merge_call(h, rg, ya, ga, gb, x, mod, w_o_rnn[l].astype(BF16), w_o_attn[l].astype(BF16),
                         w_out[l].astype(BF16), norm2_w[l][None], tm=256)
    return _ffn_call(h2, x1, mod, w_up[l].astype(BF16), ffn_conv_w[l], ffn_conv_b[l][None],
                     w_down[l].astype(BF16), final_norm_w[None], tm=512)
```

```python
import functools

import numpy as np
import jax
import jax.numpy as jnp
from jax import lax
from jax.experimental import pallas as pl
from jax.experimental.pallas import tpu as pltpu

F32 = jnp.float32
BF16 = jnp.bfloat16

D_MODEL = 1024
N_HEADS = 8
N_KV_HEADS = 2
HEAD_DIM = 128
Q_PER_KV = N_HEADS // N_KV_HEADS
ATTN_WIDTH = N_HEADS * HEAD_DIM
KV_WIDTH = N_KV_HEADS * HEAD_DIM
WINDOW = 128
BLOCK = 128
GRID_W = 64
ROPE_THETA = 10000.0
D_RNN = D_MODEL
N_LRU_BLOCKS = 16
LRU_BLOCK = D_RNN // N_LRU_BLOCKS
LRU_C = 8.0
LRU_CONV_W = 4
LRU_CONV_LEFT = 2
N_DIRS = 2
D_FF = 3 * D_MODEL
FFN_CONV_W = 3
N_MOD = 6
EPS = 1e-6
NEG_INF = -1e30

COL_R = 0
COL_RG = COL_R + D_RNN
COL_Q = COL_RG + D_RNN
COL_K = COL_Q + ATTN_WIDTH
COL_V = COL_K + KV_WIDTH
COL_GA = COL_V + KV_WIDTH
COL_GB = COL_GA + D_MODEL
D_IN = COL_GB + D_MODEL

SUBLANES = 8
BF16_SUBLANES = 16
LANES = 128
MXU_DIM = 256
VMEM_LIMIT = 56 * 1024 * 1024

LANE_GROUPS = D_RNN // LANES
GATE_TILE = MXU_DIM
GATE_TILES = D_RNN // GATE_TILE
FFN_CHUNK = 512
HALO = BF16_SUBLANES
LOG2E = float(np.log2(np.e))
TINY = 1e-30


def _gelu(v):
    return 0.5 * v * (1.0 + jnp.tanh(np.float32(np.sqrt(2.0 / np.pi)) * (v + 0.044715 * (v * v * v))))


def _sigmoid(v):
    return 0.5 * jnp.tanh(0.5 * v) + 0.5


def _const_spec(shape):
    nd = len(shape)
    return pl.BlockSpec(shape, lambda *_: (0,) * nd, pipeline_mode=pl.Buffered(1))


def _halo_specs(tm, length, width):
    hb = tm // HALO
    n_hb = length // HALO
    prev = pl.BlockSpec((1, HALO, width), lambda b, i: (b, jnp.maximum(i * hb - 1, 0), 0))
    nxt = pl.BlockSpec((1, HALO, width), lambda b, i: (b, jnp.minimum((i + 1) * hb, n_hb - 1), 0))
    return prev, nxt


def _mod_kernel(c_ref, w_ref, b_ref, o_ref):
    c = c_ref[...]
    s = c * _sigmoid(c)
    o_ref[...] = jnp.dot(s, w_ref[...], preferred_element_type=F32) + b_ref[...]


def _mod_call(cs, w_mod, b_mod):
    n_out = w_mod.shape[1]
    tn = n_out // 4
    return pl.pallas_call(
        _mod_kernel,
        grid=(n_out // tn,),
        in_specs=[pl.BlockSpec((SUBLANES, D_MODEL), lambda j: (0, 0)),
                  pl.BlockSpec((D_MODEL, tn), lambda j: (0, j)),
                  pl.BlockSpec((1, tn), lambda j: (0, j))],
        out_specs=pl.BlockSpec((SUBLANES, tn), lambda j: (0, j)),
        out_shape=jax.ShapeDtypeStruct((SUBLANES, n_out), F32),
        compiler_params=pltpu.CompilerParams(dimension_semantics=("arbitrary",),
                                             vmem_limit_bytes=VMEM_LIMIT),
        name="mod",
    )(cs, w_mod, b_mod)


def _inproj_kernel(*refs, plan, rope, tm, nb):
    x_ref, xp_ref, xn_ref, mod_ref, nw_ref, w_ref, cw_ref, cb_ref = refs[:8]
    n_in = 10 if rope else 8
    out_refs = refs[n_in:n_in + len(plan)]
    h_scr, r_scr = refs[-2:]
    i = pl.program_id(1)

    def norm_mod(x):
        ms = jnp.mean(x * x, axis=-1, keepdims=True)
        y = x * lax.rsqrt(ms + EPS) * nw_ref[...]
        return y * (1.0 + mod_ref[0, 1:2, :]) + mod_ref[0, 0:1, :]

    h_scr[pl.ds(0, HALO), :] = jnp.where(i > 0, norm_mod(xp_ref[0]), 0.0).astype(BF16)
    h_scr[pl.ds(HALO, tm), :] = norm_mod(x_ref[0]).astype(BF16)
    h_scr[pl.ds(HALO + tm, HALO), :] = jnp.where(i < nb - 1, norm_mod(xn_ref[0]), 0.0).astype(BF16)

    if rope:
        cos, sin = refs[8][...], refs[9][...]
        lane = lax.broadcasted_iota(jnp.int32, (tm, HEAD_DIM), 1)
        low = (lane % (HEAD_DIM // 2)) < (HEAD_DIM // 4)

    for (c0, width, kind), o_ref in zip(plan, out_refs):
        if kind == "conv":
            r_scr[...] = jnp.dot(h_scr[...], w_ref[:, c0:c0 + width], preferred_element_type=F32)
            xc = 0.5 * cb_ref[...] + sum((0.5 * cw_ref[k:k + 1, :]) * r_scr[pl.ds(HALO - LRU_CONV_LEFT + k, tm), :]
                                         for k in range(LRU_CONV_W))
            o_ref[0] = xc
            continue
        res = jnp.dot(h_scr[pl.ds(HALO, tm), :], w_ref[:, c0:c0 + width], preferred_element_type=F32)
        if kind in ("rope_q", "rope_k"):
            scale = HEAD_DIM ** -0.5 if kind == "rope_q" else None
            for hh in range(width // HEAD_DIM):
                sl = slice(hh * HEAD_DIM, (hh + 1) * HEAD_DIM)
                v = res[:, sl]
                partner = jnp.where(low, pltpu.roll(v, HEAD_DIM - HEAD_DIM // 4, 1), pltpu.roll(v, HEAD_DIM // 4, 1))
                rot = v * cos + partner * sin
                if scale is not None:
                    rot = rot * scale
                o_ref[0, :, sl] = rot.astype(o_ref.dtype)
        else:
            o_ref[0] = res.astype(o_ref.dtype)


def _inproj_call(x, mod, mod_row0, nw, w_in, conv_w, conv_b, tables, plan, tm):
    bsz, length, _ = x.shape
    rope = tables is not None
    nb = length // tm
    tok = lambda b, i: (b, i, 0)
    prev, nxt = _halo_specs(tm, length, D_MODEL)
    in_specs = [pl.BlockSpec((1, tm, D_MODEL), tok), prev, nxt,
                pl.BlockSpec((1, N_MOD, D_MODEL), lambda b, i: (b * (mod_row0 == 0) + mod_row0, 0, 0)),
                _const_spec((1, D_MODEL)),
                _const_spec((D_MODEL, D_IN)),
                _const_spec((LRU_CONV_W, D_RNN)),
                _const_spec((1, D_RNN))]
    args = [x, x, x, mod, nw, w_in, conv_w, conv_b]
    if rope:
        in_specs += [pl.BlockSpec((tm, HEAD_DIM), lambda b, i: (i, 0))] * 2
        args += list(tables)
    out_specs, out_shape = [], []
    for (_, width, kind) in plan:
        dt = F32 if kind == "conv" else BF16
        out_specs.append(pl.BlockSpec((1, tm, width), tok))
        out_shape.append(jax.ShapeDtypeStruct((bsz, length, width), dt))
    return pl.pallas_call(
        functools.partial(_inproj_kernel, plan=tuple(plan), rope=rope, tm=tm, nb=nb),
        grid=(bsz, nb),
        in_specs=in_specs, out_specs=out_specs, out_shape=out_shape,
        scratch_shapes=[pltpu.VMEM((tm + 2 * HALO, D_MODEL), BF16),
                        pltpu.VMEM((tm + 2 * HALO, D_RNN), F32)],
        compiler_params=pltpu.CompilerParams(dimension_semantics=("parallel", "parallel"),
                                             vmem_limit_bytes=VMEM_LIMIT),
        name="inproj_x" if rope else "inproj_ctx",
    )(*args)


def _scan_kernel(xf_ref, xb_ref, xc_ref, wg_ref, bg_ref, lam_ref, hf_ref, hb_ref,
                 a3, b3, h3, hc_scr, *, tm, nb, clen, bsz):
    i = pl.program_id(0)
    chains = [(d, b) for d in range(N_DIRS) for b in range(bsz)]

    def gates(src_ref, d, b, ch, n):
        lam = lam_ref[d]
        softplus = jnp.maximum(-lam, 0.0) + jnp.log(1.0 + jnp.exp(-jnp.abs(lam)))
        hd2 = (-0.5 * LRU_C * LOG2E) * softplus
        for kt in range(GATE_TILES):
            cs = slice(kt * GATE_TILE, (kt + 1) * GATE_TILE)
            xk = src_ref[b, pl.ds(0, n), cs]
            g = jnp.dot(xk.astype(BF16), wg_ref[d, kt], preferred_element_type=F32)
            t_r = jnp.tanh(g[:, :GATE_TILE] + 0.5 * bg_ref[d, 0:1, cs])
            t_i = jnp.tanh(g[:, GATE_TILE:] + 0.5 * bg_ref[d, 1:2, cs])
            a = jnp.exp2(hd2[:, cs] * t_r + hd2[:, cs])
            y = 1.0 - a * a
            bb = (y * lax.rsqrt(jnp.maximum(y, TINY)) * xk) * (t_i + 1.0)
            for gg in range(GATE_TILE // LANES):
                lg = kt * (GATE_TILE // LANES) + gg
                ls = slice(gg * LANES, (gg + 1) * LANES)
                a3[ch, pl.ds(0, n // SUBLANES), lg * SUBLANES:(lg + 1) * SUBLANES, :] = (
                    a[:, ls].reshape(n // SUBLANES, SUBLANES, LANES))
                b3[ch, pl.ds(0, n // SUBLANES), lg * SUBLANES:(lg + 1) * SUBLANES, :] = (
                    bb[:, ls].reshape(n // SUBLANES, SUBLANES, LANES))

    def run_scan(n, h0s, keep):
        n8 = n // SUBLANES

        def body(tau, hs):
            hs = list(hs)
            for rho in range(SUBLANES):
                for ch, (d, b) in enumerate(chains):
                    tt, rr = (tau, rho) if d == 0 else (n8 - 1 - tau, SUBLANES - 1 - rho)
                    win = pl.ds(rr, LANE_GROUPS, stride=SUBLANES)
                    h = a3[ch, tt, win, :] * hs[ch] + b3[ch, tt, win, :]
                    hs[ch] = h
                    if keep:
                        h3[ch, tt, win, :] = h
            return tuple(hs)

        return lax.fori_loop(0, n8, body, tuple(h0s))

    @pl.when(i == 0)
    def _():
        for ch, (d, b) in enumerate(chains):
            gates(xc_ref, d, b, ch, clen)
        hs = run_scan(clen, [jnp.zeros((LANE_GROUPS, LANES), F32)] * len(chains), False)
        for ch in range(len(chains)):
            hc_scr[ch] = hs[ch]

    for ch, (d, b) in enumerate(chains):
        gates(xf_ref if d == 0 else xb_ref, d, b, ch, tm)
    hs = run_scan(tm, [hc_scr[ch] for ch in range(len(chains))], True)
    for ch, (d, b) in enumerate(chains):
        hc_scr[ch] = hs[ch]
        o_ref = hf_ref if d == 0 else hb_ref
        for lg in range(LANE_GROUPS):
            slab = h3[ch, pl.ds(0, tm // SUBLANES), lg * SUBLANES:(lg + 1) * SUBLANES, :]
            o_ref[b, :, lg * LANES:(lg + 1) * LANES] = slab.reshape(tm, LANES).astype(o_ref.dtype)


def _scan_call(xc, xc_ctx, wg, bg, lam, tm):
    bsz, length, ch = xc.shape
    clen = xc_ctx.shape[1]
    nb = length // tm
    n_chains = N_DIRS * bsz
    rows8 = max(tm, clen) // SUBLANES
    slab = pltpu.VMEM((n_chains, rows8, LANE_GROUPS * SUBLANES, LANES), F32)
    fwd = pl.BlockSpec((bsz, tm, ch), lambda i: (0, i, 0))
    bwd = pl.BlockSpec((bsz, tm, ch), lambda i: (0, nb - 1 - i, 0))
    return pl.pallas_call(
        functools.partial(_scan_kernel, tm=tm, nb=nb, clen=clen, bsz=bsz),
        grid=(nb,),
        in_specs=[fwd, bwd,
                  _const_spec((bsz, clen, ch)),
                  _const_spec((N_DIRS, GATE_TILES, GATE_TILE, 2 * GATE_TILE)),
                  _const_spec((N_DIRS, 2, ch)),
                  _const_spec((N_DIRS, 1, ch))],
        out_specs=[fwd, bwd],
        out_shape=[jax.ShapeDtypeStruct((bsz, length, ch), BF16)] * 2,
        scratch_shapes=[slab, slab, slab, pltpu.VMEM((n_chains, LANE_GROUPS, LANES), F32)],
        compiler_params=pltpu.CompilerParams(dimension_semantics=("arbitrary",),
                                             vmem_limit_bytes=VMEM_LIMIT),
        name="scan",
    )(xc, xc, xc_ctx, wg, bg, lam)


def _attn_kernel(sink_ref, q_ref, k_ref, kp_ref, kn_ref, v_ref, vp_ref, vn_ref, kc_ref, vc_ref,
                 o_ref, kcat, vcat, vctx, *, nqb, length):
    i = pl.program_id(1)
    tq = nqb * BLOCK
    clen = kc_ref.shape[1]
    kcat[pl.ds(0, BLOCK), :] = kp_ref[0]
    kcat[pl.ds(BLOCK, tq), :] = k_ref[0]
    kcat[pl.ds(BLOCK + tq, BLOCK), :] = kn_ref[0]
    for g in range(N_KV_HEADS):
        gs = slice(g * HEAD_DIM, (g + 1) * HEAD_DIM)
        vs = slice(2 * g * HEAD_DIM, (2 * g + 1) * HEAD_DIM)
        os_ = slice((2 * g + 1) * HEAD_DIM, (2 * g + 2) * HEAD_DIM)
        vcat[pl.ds(0, BLOCK), vs] = vp_ref[0, :, gs]
        vcat[pl.ds(BLOCK, tq), vs] = v_ref[0, :, gs]
        vcat[pl.ds(BLOCK + tq, BLOCK), vs] = vn_ref[0, :, gs]
        vcat[:, os_] = jnp.ones((tq + 2 * BLOCK, HEAD_DIM), BF16)
        vctx[:, vs] = vc_ref[0, :, gs]
        vctx[:, os_] = jnp.ones((clen, HEAD_DIM), BF16)

    qi = lax.broadcasted_iota(jnp.int32, (BLOCK, 3 * BLOCK), 0)
    kj = lax.broadcasted_iota(jnp.int32, (BLOCK, 3 * BLOCK), 1)
    rel = kj - qi
    band = (rel >= BLOCK - WINDOW) & (rel <= BLOCK + WINDOW)
    nt = (((1,), (1,)), ((), ()))
    rows = Q_PER_KV * BLOCK
    for jq in range(nqb):
        kpos = (i * nqb + jq - 1) * BLOCK + kj
        mask = (band & (kpos >= 0) & (kpos < length))[None]
        for g in range(N_KV_HEADS):
            gs = slice(g * HEAD_DIM, (g + 1) * HEAD_DIM)
            vos = slice(2 * g * HEAD_DIM, (2 * g + 2) * HEAD_DIM)
            heads = [g * Q_PER_KV + hh for hh in range(Q_PER_KV)]
            q4 = jnp.concatenate(
                [q_ref[0, pl.ds(jq * BLOCK, BLOCK), h * HEAD_DIM:(h + 1) * HEAD_DIM] for h in heads], axis=0)
            sink = jnp.concatenate([jnp.full((BLOCK, 1), sink_ref[h], F32) for h in heads], axis=0)
            s_loc = lax.dot_general(q4, kcat[pl.ds(jq * BLOCK, 3 * BLOCK), gs], nt, preferred_element_type=F32)
            s_loc = jnp.where(mask, s_loc.reshape(Q_PER_KV, BLOCK, 3 * BLOCK), NEG_INF).reshape(rows, 3 * BLOCK)
            s_ctx = lax.dot_general(q4, kc_ref[0, :, gs], nt, preferred_element_type=F32)
            cols = ([s_loc[:, c * LANES:(c + 1) * LANES] for c in range(3 * BLOCK // LANES)]
                    + [s_ctx[:, c * LANES:(c + 1) * LANES] for c in range(clen // LANES)])
            m = jnp.maximum(jnp.max(functools.reduce(jnp.maximum, cols), axis=-1, keepdims=True), sink)
            p_loc = jnp.exp(s_loc - m).astype(BF16)
            p_ctx = jnp.exp(s_ctx - m).astype(BF16)
            o2 = (jnp.dot(p_loc, vcat[pl.ds(jq * BLOCK, 3 * BLOCK), vos], preferred_element_type=F32)
                  + jnp.dot(p_ctx, vctx[:, vos], preferred_element_type=F32))
            denom = o2[:, HEAD_DIM:] + jnp.exp(sink - m)
            o = o2[:, :HEAD_DIM] * (1.0 / denom)
            for hh, h in enumerate(heads):
                o_ref[0, pl.ds(jq * BLOCK, BLOCK), h * HEAD_DIM:(h + 1) * HEAD_DIM] = (
                    o[hh * BLOCK:(hh + 1) * BLOCK].astype(o_ref.dtype))


def _attn_call(sink, q, k, v, kc, vc, nqb):
    bsz, length, _ = q.shape
    clen = kc.shape[1]
    tq = nqb * BLOCK
    n_blk = length // BLOCK
    prev = lambda b, i: (b, jnp.maximum(i * nqb - 1, 0), 0)
    nxt = lambda b, i: (b, jnp.minimum((i + 1) * nqb, n_blk - 1), 0)
    main = lambda b, i: (b, i, 0)
    kv_specs = [pl.BlockSpec((1, tq, KV_WIDTH), main),
                pl.BlockSpec((1, BLOCK, KV_WIDTH), prev),
                pl.BlockSpec((1, BLOCK, KV_WIDTH), nxt)]
    ctx_spec = pl.BlockSpec((1, clen, KV_WIDTH), lambda b, i: (b, 0, 0))
    return pl.pallas_call(
        functools.partial(_attn_kernel, nqb=nqb, length=length),
        grid=(bsz, length // tq),
        in_specs=[pl.BlockSpec(memory_space=pltpu.SMEM),
                  pl.BlockSpec((1, tq, ATTN_WIDTH), main)] + kv_specs + kv_specs + [ctx_spec, ctx_spec],
        out_specs=pl.BlockSpec((1, tq, ATTN_WIDTH), main),
        out_shape=jax.ShapeDtypeStruct((bsz, length, ATTN_WIDTH), BF16),
        scratch_shapes=[pltpu.VMEM((tq + 2 * BLOCK, KV_WIDTH), BF16),
                        pltpu.VMEM((tq + 2 * BLOCK, 2 * KV_WIDTH), BF16),
                        pltpu.VMEM((clen, 2 * KV_WIDTH), BF16)],
        compiler_params=pltpu.CompilerParams(dimension_semantics=("parallel", "parallel"),
                                             vmem_limit_bytes=VMEM_LIMIT),
        name="attn",
    )(sink, q, k, k, k, v, v, v, kc, vc)


def _merge_kernel(hf_ref, hb_ref, rg_ref, ya_ref, ga_ref, gb_ref, x_ref, mod_ref,
                  wr_ref, wa_ref, wo_ref, nw_ref, x1_ref, h2_ref):
    y_rnn = (hf_ref[0].astype(F32) + hb_ref[0].astype(F32)) * _gelu(rg_ref[0].astype(F32))
    t_rnn = jnp.dot(y_rnn.astype(BF16), wr_ref[...], preferred_element_type=F32)
    t_att = jnp.dot(ya_ref[0], wa_ref[...], preferred_element_type=F32)
    y = _sigmoid(ga_ref[0].astype(F32)) * t_rnn + _sigmoid(gb_ref[0].astype(F32)) * t_att
    z = jnp.dot(y.astype(BF16), wo_ref[...], preferred_element_type=F32)
    x1 = x_ref[0] + mod_ref[0, 2:3, :] * z
    x1_ref[0] = x1
    ms = jnp.mean(x1 * x1, axis=-1, keepdims=True)
    n2 = x1 * lax.rsqrt(ms + EPS) * nw_ref[...]
    h2_ref[0] = (n2 * (1.0 + mod_ref[0, 4:5, :]) + mod_ref[0, 3:4, :]).astype(h2_ref.dtype)


def _merge_call(hf, hb, rg, ya, ga, gb, x, mod, w_o_rnn, w_o_attn, w_out, nw2, tm):
    bsz, length, _ = x.shape
    blk = pl.BlockSpec((1, tm, D_MODEL), lambda b, i: (b, i, 0))
    return pl.pallas_call(
        _merge_kernel,
        grid=(bsz, length // tm),
        in_specs=[blk, blk, blk, blk, blk, blk, blk,
                  pl.BlockSpec((1, N_MOD, D_MODEL), lambda b, i: (b, 0, 0)),
                  _const_spec((D_RNN, D_MODEL)), _const_spec((ATTN_WIDTH, D_MODEL)),
                  _const_spec((D_MODEL, D_MODEL)), _const_spec((1, D_MODEL))],
        out_specs=[blk, blk],
        out_shape=[jax.ShapeDtypeStruct((bsz, length, D_MODEL), F32),
                   jax.ShapeDtypeStruct((bsz, length, D_MODEL), BF16)],
        compiler_params=pltpu.CompilerParams(dimension_semantics=("parallel", "parallel"),
                                             vmem_limit_bytes=VMEM_LIMIT),
        name="merge",
    )(hf, hb, rg, ya, ga, gb, x, mod, w_o_rnn, w_o_attn, w_out, nw2)


def _ffn_kernel(h_ref, hp_ref, hn_ref, x1_ref, mod_ref, wu_ref, cw_ref, cb_ref, wd_ref, fw_ref,
                o_ref, hbuf, ua_scr, uv_scr, g_scr, *, tm, nb):
    i = pl.program_id(1)
    hbuf[pl.ds(0, HALO), :] = jnp.where(i > 0, hp_ref[0], jnp.zeros_like(hp_ref[0]))
    hbuf[pl.ds(HALO, tm), :] = h_ref[0]
    hbuf[pl.ds(HALO + tm, HALO), :] = jnp.where(i < nb - 1, hn_ref[0], jnp.zeros_like(hn_ref[0]))

    def conv(u_ref, c0):
        cs = slice(c0, c0 + FFN_CHUNK)
        return cb_ref[:, cs] + sum(cw_ref[k:k + 1, cs] * u_ref[pl.ds(HALO - FFN_CONV_W // 2 + k, tm), :]
                                   for k in range(FFN_CONV_W))

    for c in range(D_FF // FFN_CHUNK):
        ca, cv = c * FFN_CHUNK, D_FF + c * FFN_CHUNK
        ua, uv = ua_scr.at[c % 2], uv_scr.at[c % 2]
        ua[...] = jnp.dot(hbuf[...], wu_ref[:, ca:ca + FFN_CHUNK], preferred_element_type=F32)
        uv[...] = jnp.dot(hbuf[...], wu_ref[:, cv:cv + FFN_CHUNK], preferred_element_type=F32)
        g_scr[:, ca:ca + FFN_CHUNK] = (_gelu(conv(ua, ca)) * conv(uv, cv)).astype(BF16)
    z = jnp.dot(g_scr[...], wd_ref[...], preferred_element_type=F32)
    x2 = x1_ref[0] + mod_ref[0, 5:6, :] * z
    ms = jnp.mean(x2 * x2, axis=-1, keepdims=True)
    o_ref[0] = x2 * lax.rsqrt(ms + EPS) * fw_ref[...]


def _ffn_call(h2, x1, mod, w_up, conv_w, conv_b, w_down, fw, tm):
    bsz, length, _ = x1.shape
    nb = length // tm
    tok = lambda b, i: (b, i, 0)
    prev, nxt = _halo_specs(tm, length, D_MODEL)
    return pl.pallas_call(
        functools.partial(_ffn_kernel, tm=tm, nb=nb),
        grid=(bsz, nb),
        in_specs=[pl.BlockSpec((1, tm, D_MODEL), tok), prev, nxt,
                  pl.BlockSpec((1, tm, D_MODEL), tok),
                  pl.BlockSpec((1, N_MOD, D_MODEL), lambda b, i: (b, 0, 0)),
                  _const_spec((D_MODEL, 2 * D_FF)), _const_spec((FFN_CONV_W, 2 * D_FF)),
                  _const_spec((1, 2 * D_FF)), _const_spec((D_FF, D_MODEL)), _const_spec((1, D_MODEL))],
        out_specs=pl.BlockSpec((1, tm, D_MODEL), tok),
        out_shape=jax.ShapeDtypeStruct((bsz, length, D_MODEL), F32),
        scratch_shapes=[pltpu.VMEM((tm + 2 * HALO, D_MODEL), BF16),
                        pltpu.VMEM((2, tm + 2 * HALO, FFN_CHUNK), F32),
                        pltpu.VMEM((2, tm + 2 * HALO, FFN_CHUNK), F32),
                        pltpu.VMEM((tm, D_FF), BF16)],
        compiler_params=pltpu.CompilerParams(dimension_semantics=("parallel", "arbitrary"),
                                             vmem_limit_bytes=VMEM_LIMIT),
        name="ffn",
    )(h2, h2, h2, x1, mod, w_up, conv_w, conv_b, w_down, fw)


def _rope_tables(length):
    pos = np.arange(length)
    row, col = pos // GRID_W, pos % GRID_W
    half = HEAD_DIM // 2
    inv = ROPE_THETA ** (-np.arange(0, half, 2, dtype=np.float64) / half)
    ar, ac = row[:, None] * inv, col[:, None] * inv
    cos = np.concatenate([np.cos(ar), np.cos(ar), np.cos(ac), np.cos(ac)], axis=1)
    sin = np.concatenate([-np.sin(ar), np.sin(ar), -np.sin(ac), np.sin(ac)], axis=1)
    return jnp.asarray(cos, F32), jnp.asarray(sin, F32)


def _gate_weights(w_a, w_x):
    per = GATE_TILE // LRU_BLOCK
    eye = jnp.eye(per, dtype=F32)

    def bd(w):
        w = w.reshape(N_DIRS, GATE_TILES, per, LRU_BLOCK, LRU_BLOCK)
        return jnp.einsum('dtpcn,pq->dtpcqn', w, eye).reshape(N_DIRS, GATE_TILES, GATE_TILE, GATE_TILE)

    return jnp.concatenate([bd(w_a), bd(w_x)], axis=-1).astype(BF16)


def kernel(x, c, ctx, c_ctx, w_mod, b_mod, norm1_w, w_in, lru_conv_w, lru_conv_b, gate_a_w, gate_a_b,
           gate_x_w, gate_x_b, lru_lambda, sink_logit, w_o_rnn, w_o_attn, w_out, norm2_w, w_up,
           ffn_conv_w, ffn_conv_b, w_down, final_norm_w):
    bsz, length, _ = x.shape
    l = 0

    cs = jnp.concatenate([c, c_ctx[None], jnp.zeros((SUBLANES - bsz - 1, D_MODEL), F32)], axis=0)
    mod = _mod_call(cs, w_mod[l], b_mod[l][None]).reshape(SUBLANES, N_MOD, D_MODEL)

    w_in_b = w_in[l].astype(BF16)
    nw1 = norm1_w[l][None]
    cw, cb = lru_conv_w[l], lru_conv_b[l][None]

    plan_x = [(COL_R, D_RNN, "conv"), (COL_RG, D_RNN, "bf16"), (COL_Q, ATTN_WIDTH, "rope_q"),
              (COL_K, KV_WIDTH, "rope_k"), (COL_V, KV_WIDTH, "bf16"), (COL_GA, D_MODEL, "bf16"),
              (COL_GB, D_MODEL, "bf16")]
    xc, rg, q, k, v, ga, gb = _inproj_call(x, mod, 0, nw1, w_in_b, cw, cb, _rope_tables(length), plan_x, tm=512)
    plan_c = [(COL_R, D_RNN, "conv"), (COL_K, KV_WIDTH, "bf16"), (COL_V, KV_WIDTH, "bf16")]
    xc_ctx, kc, vc = _inproj_call(ctx, mod, bsz, nw1, w_in_b, cw, cb, None, plan_c, tm=ctx.shape[1])

    wg = _gate_weights(gate_a_w[l], gate_x_w[l])
    bg = jnp.stack([gate_a_b[l], gate_x_b[l]], axis=1)
    hf, hb = _scan_call(xc, xc_ctx, wg, bg, lru_lambda[l][:, None, :], tm=256)

    ya = _attn_call(sink_logit[l], q, k, v, kc, vc, nqb=4)

    x1, h2 = _merge_call(hf, hb, rg, ya, ga, gb, x, mod, w_o_rnn[l].astype(BF16), w_o_attn[l].astype(BF16),
                         w_out[l].astype(BF16), norm2_w[l][None], tm=512)
    return _ffn_call(h2, x1, mod, w_up[l].astype(BF16), ffn_conv_w[l], ffn_conv_b[l][None],
                     w_down[l].astype(BF16), final_norm_w[None], tm=512)
```

```python
import functools

import numpy as np
import jax
import jax.numpy as jnp
from jax import lax
from jax.experimental import pallas as pl
from jax.experimental.pallas import tpu as pltpu

F32 = jnp.float32
BF16 = jnp.bfloat16

D_MODEL = 1024
N_HEADS = 8
N_KV_HEADS = 2
HEAD_DIM = 128
Q_PER_KV = N_HEADS // N_KV_HEADS
ATTN_WIDTH = N_HEADS * HEAD_DIM
KV_WIDTH = N_KV_HEADS * HEAD_DIM
WINDOW = 128
BLOCK = 128
GRID_W = 64
ROPE_THETA = 10000.0
D_RNN = D_MODEL
N_LRU_BLOCKS = 16
LRU_BLOCK = D_RNN // N_LRU_BLOCKS
LRU_C = 8.0
LRU_CONV_W = 4
LRU_CONV_LEFT = 2
N_DIRS = 2
D_FF = 3 * D_MODEL
FFN_CONV_W = 3
N_MOD = 6
EPS = 1e-6
NEG_INF = -1e30

COL_R = 0
COL_RG = COL_R + D_RNN
COL_Q = COL_RG + D_RNN
COL_K = COL_Q + ATTN_WIDTH
COL_V = COL_K + KV_WIDTH
COL_GA = COL_V + KV_WIDTH
COL_GB = COL_GA + D_MODEL
D_IN = COL_GB + D_MODEL

SUBLANES = 8
BF16_SUBLANES = 16
LANES = 128
MXU_DIM = 256
VMEM_LIMIT = 56 * 1024 * 1024

LANE_GROUPS = D_RNN // LANES
GATE_TILE = MXU_DIM
GATE_TILES = D_RNN // GATE_TILE
FFN_CHUNK = 512
HALO = BF16_SUBLANES
LOG2E = float(np.log2(np.e))
TINY = 1e-30


GELU_K1 = float(np.sqrt(2.0 / np.pi))
GELU_K2 = GELU_K1 * 0.044715


def _gelu(v):
    return (0.5 * v) * (1.0 + jnp.tanh(v * (GELU_K1 + GELU_K2 * (v * v))))


def _sigmoid(v):
    return 0.5 * jnp.tanh(0.5 * v) + 0.5


def _const_spec(shape):
    nd = len(shape)
    return pl.BlockSpec(shape, lambda *_: (0,) * nd, pipeline_mode=pl.Buffered(1))


def _halo_specs(tm, length, width):
    hb = tm // HALO
    n_hb = length // HALO
    prev = pl.BlockSpec((1, HALO, width), lambda b, i: (b, jnp.maximum(i * hb - 1, 0), 0))
    nxt = pl.BlockSpec((1, HALO, width), lambda b, i: (b, jnp.minimum((i + 1) * hb, n_hb - 1), 0))
    return prev, nxt


def _mod_kernel(c_ref, w_ref, b_ref, o_ref):
    c = c_ref[...]
    s = c * _sigmoid(c)
    o_ref[...] = jnp.dot(s, w_ref[...], preferred_element_type=F32) + b_ref[...]


def _mod_call(cs, w_mod, b_mod):
    n_out = w_mod.shape[1]
    tn = n_out // 4
    return pl.pallas_call(
        _mod_kernel,
        grid=(n_out // tn,),
        in_specs=[pl.BlockSpec((SUBLANES, D_MODEL), lambda j: (0, 0)),
                  pl.BlockSpec((D_MODEL, tn), lambda j: (0, j)),
                  pl.BlockSpec((1, tn), lambda j: (0, j))],
        out_specs=pl.BlockSpec((SUBLANES, tn), lambda j: (0, j)),
        out_shape=jax.ShapeDtypeStruct((SUBLANES, n_out), F32),
        compiler_params=pltpu.CompilerParams(dimension_semantics=("arbitrary",),
                                             vmem_limit_bytes=VMEM_LIMIT),
        name="mod",
    )(cs, w_mod, b_mod)


def _inproj_kernel(*refs, plan, rope, tm, nb):
    x_ref, xp_ref, xn_ref, mod_ref, nw_ref, w_ref, cw_ref, cb_ref = refs[:8]
    n_in = 10 if rope else 8
    out_refs = refs[n_in:n_in + len(plan)]
    h_scr, r_scr = refs[-2:]
    i = pl.program_id(1)

    def norm_mod(x):
        ms = jnp.mean(x * x, axis=-1, keepdims=True)
        y = x * lax.rsqrt(ms + EPS) * nw_ref[...]
        return y * (1.0 + mod_ref[0, 1:2, :]) + mod_ref[0, 0:1, :]

    h_scr[pl.ds(0, HALO), :] = jnp.where(i > 0, norm_mod(xp_ref[0]), 0.0).astype(BF16)
    h_scr[pl.ds(HALO, tm), :] = norm_mod(x_ref[0]).astype(BF16)
    h_scr[pl.ds(HALO + tm, HALO), :] = jnp.where(i < nb - 1, norm_mod(xn_ref[0]), 0.0).astype(BF16)

    if rope:
        cos, sin = refs[8][...], refs[9][...]
        lane = lax.broadcasted_iota(jnp.int32, (tm, HEAD_DIM), 1)
        low = (lane % (HEAD_DIM // 2)) < (HEAD_DIM // 4)

    for (c0, width, kind), o_ref in zip(plan, out_refs):
        if kind == "conv":
            r_scr[...] = jnp.dot(h_scr[...], w_ref[:, c0:c0 + width], preferred_element_type=F32)
            xc = 0.5 * cb_ref[...] + sum((0.5 * cw_ref[k:k + 1, :]) * r_scr[pl.ds(HALO - LRU_CONV_LEFT + k, tm), :]
                                         for k in range(LRU_CONV_W))
            o_ref[0] = xc
            continue
        res = jnp.dot(h_scr[pl.ds(HALO, tm), :], w_ref[:, c0:c0 + width], preferred_element_type=F32)
        if kind in ("rope_q", "rope_k"):
            scale = HEAD_DIM ** -0.5 if kind == "rope_q" else None
            for hh in range(width // HEAD_DIM):
                sl = slice(hh * HEAD_DIM, (hh + 1) * HEAD_DIM)
                v = res[:, sl]
                partner = jnp.where(low, pltpu.roll(v, HEAD_DIM - HEAD_DIM // 4, 1), pltpu.roll(v, HEAD_DIM // 4, 1))
                rot = v * cos + partner * sin
                if scale is not None:
                    rot = rot * scale
                o_ref[0, :, sl] = rot.astype(o_ref.dtype)
        else:
            o_ref[0] = res.astype(o_ref.dtype)


def _inproj_call(x, mod, mod_row0, nw, w_in, conv_w, conv_b, tables, plan, tm):
    bsz, length, _ = x.shape
    rope = tables is not None
    nb = length // tm
    tok = lambda b, i: (b, i, 0)
    prev, nxt = _halo_specs(tm, length, D_MODEL)
    in_specs = [pl.BlockSpec((1, tm, D_MODEL), tok), prev, nxt,
                pl.BlockSpec((1, N_MOD, D_MODEL), lambda b, i: (b * (mod_row0 == 0) + mod_row0, 0, 0)),
                _const_spec((1, D_MODEL)),
                _const_spec((D_MODEL, D_IN)),
                _const_spec((LRU_CONV_W, D_RNN)),
                _const_spec((1, D_RNN))]
    args = [x, x, x, mod, nw, w_in, conv_w, conv_b]
    if rope:
        in_specs += [pl.BlockSpec((tm, HEAD_DIM), lambda b, i: (i, 0))] * 2
        args += list(tables)
    out_specs, out_shape = [], []
    for (_, width, kind) in plan:
        dt = F32 if kind == "conv" else BF16
        out_specs.append(pl.BlockSpec((1, tm, width), tok))
        out_shape.append(jax.ShapeDtypeStruct((bsz, length, width), dt))
    return pl.pallas_call(
        functools.partial(_inproj_kernel, plan=tuple(plan), rope=rope, tm=tm, nb=nb),
        grid=(bsz, nb),
        in_specs=in_specs, out_specs=out_specs, out_shape=out_shape,
        scratch_shapes=[pltpu.VMEM((tm + 2 * HALO, D_MODEL), BF16),
                        pltpu.VMEM((tm + 2 * HALO, D_RNN), F32)],
        compiler_params=pltpu.CompilerParams(dimension_semantics=("parallel", "parallel"),
                                             vmem_limit_bytes=VMEM_LIMIT),
        name="inproj_x" if rope else "inproj_ctx",
    )(*args)


def _scan_kernel(xf_ref, xb_ref, xc_ref, wg_ref, bg_ref, lam_ref, hf_ref, hb_ref,
                 a3, b3, h3, hc_scr, *, tm, nb, clen, bsz):
    i = pl.program_id(0)
    chains = [(d, b) for d in range(N_DIRS) for b in range(bsz)]

    def gates(src_ref, d, b, ch, n):
        lam = lam_ref[d]
        softplus = jnp.maximum(-lam, 0.0) + jnp.log(1.0 + jnp.exp(-jnp.abs(lam)))
        hd2 = (-0.5 * LRU_C * LOG2E) * softplus
        for kt in range(GATE_TILES):
            cs = slice(kt * GATE_TILE, (kt + 1) * GATE_TILE)
            xk = src_ref[b, pl.ds(0, n), cs]
            g = jnp.dot(xk.astype(BF16), wg_ref[d, kt], preferred_element_type=F32)
            t_r = jnp.tanh(g[:, :GATE_TILE] + 0.5 * bg_ref[d, 0:1, cs])
            t_i = jnp.tanh(g[:, GATE_TILE:] + 0.5 * bg_ref[d, 1:2, cs])
            a = jnp.exp2(hd2[:, cs] * t_r + hd2[:, cs])
            y = 1.0 - a * a
            bb = (y * lax.rsqrt(jnp.maximum(y, TINY)) * xk) * (t_i + 1.0)
            for gg in range(GATE_TILE // LANES):
                lg = kt * (GATE_TILE // LANES) + gg
                ls = slice(gg * LANES, (gg + 1) * LANES)
                a3[ch, pl.ds(0, n // SUBLANES), lg * SUBLANES:(lg + 1) * SUBLANES, :] = (
                    a[:, ls].reshape(n // SUBLANES, SUBLANES, LANES))
                b3[ch, pl.ds(0, n // SUBLANES), lg * SUBLANES:(lg + 1) * SUBLANES, :] = (
                    bb[:, ls].reshape(n // SUBLANES, SUBLANES, LANES))

    def run_scan(n, h0s, keep):
        n8 = n // SUBLANES

        def body(tau, hs):
            hs = list(hs)
            for rho in range(SUBLANES):
                for ch, (d, b) in enumerate(chains):
                    tt, rr = (tau, rho) if d == 0 else (n8 - 1 - tau, SUBLANES - 1 - rho)
                    win = pl.ds(rr, LANE_GROUPS, stride=SUBLANES)
                    h = a3[ch, tt, win, :] * hs[ch] + b3[ch, tt, win, :]
                    hs[ch] = h
                    if keep:
                        h3[ch, tt, win, :] = h
            return tuple(hs)

        return lax.fori_loop(0, n8, body, tuple(h0s))

    @pl.when(i == 0)
    def _():
        for ch, (d, b) in enumerate(chains):
            gates(xc_ref, d, b, ch, clen)
        hs = run_scan(clen, [jnp.zeros((LANE_GROUPS, LANES), F32)] * len(chains), False)
        for ch in range(len(chains)):
            hc_scr[ch] = hs[ch]

    for ch, (d, b) in enumerate(chains):
        gates(xf_ref if d == 0 else xb_ref, d, b, ch, tm)
    hs = run_scan(tm, [hc_scr[ch] for ch in range(len(chains))], True)
    for ch, (d, b) in enumerate(chains):
        hc_scr[ch] = hs[ch]
        o_ref = hf_ref if d == 0 else hb_ref
        for lg in range(LANE_GROUPS):
            slab = h3[ch, pl.ds(0, tm // SUBLANES), lg * SUBLANES:(lg + 1) * SUBLANES, :]
            o_ref[b, :, lg * LANES:(lg + 1) * LANES] = slab.reshape(tm, LANES).astype(o_ref.dtype)


def _scan_call(xc, xc_ctx, wg, bg, lam, tm):
    bsz, length, ch = xc.shape
    clen = xc_ctx.shape[1]
    nb = length // tm
    n_chains = N_DIRS * bsz
    rows8 = max(tm, clen) // SUBLANES
    slab = pltpu.VMEM((n_chains, rows8, LANE_GROUPS * SUBLANES, LANES), F32)
    fwd = pl.BlockSpec((bsz, tm, ch), lambda i: (0, i, 0))
    bwd = pl.BlockSpec((bsz, tm, ch), lambda i: (0, nb - 1 - i, 0))
    return pl.pallas_call(
        functools.partial(_scan_kernel, tm=tm, nb=nb, clen=clen, bsz=bsz),
        grid=(nb,),
        in_specs=[fwd, bwd,
                  _const_spec((bsz, clen, ch)),
                  _const_spec((N_DIRS, GATE_TILES, GATE_TILE, 2 * GATE_TILE)),
                  _const_spec((N_DIRS, 2, ch)),
                  _const_spec((N_DIRS, 1, ch))],
        out_specs=[fwd, bwd],
        out_shape=[jax.ShapeDtypeStruct((bsz, length, ch), BF16)] * 2,
        scratch_shapes=[slab, slab, slab, pltpu.VMEM((n_chains, LANE_GROUPS, LANES), F32)],
        compiler_params=pltpu.CompilerParams(dimension_semantics=("arbitrary",),
                                             vmem_limit_bytes=VMEM_LIMIT),
        name="scan",
    )(xc, xc, xc_ctx, wg, bg, lam)


def _attention(i, sink_ref, q_ref, k_ref, kp_ref, kn_ref, v_ref, vp_ref, vn_ref, kc_ref, vc_ref,
               ya_scr, s_scr, kcat, vcat, vctx, *, nqb, length):
    tq = nqb * BLOCK
    clen = kc_ref.shape[1]
    kcat[pl.ds(0, BLOCK), :] = kp_ref[0]
    kcat[pl.ds(BLOCK, tq), :] = k_ref[0]
    kcat[pl.ds(BLOCK + tq, BLOCK), :] = kn_ref[0]
    for g in range(N_KV_HEADS):
        gs = slice(g * HEAD_DIM, (g + 1) * HEAD_DIM)
        vs = slice(2 * g * HEAD_DIM, (2 * g + 1) * HEAD_DIM)
        os_ = slice((2 * g + 1) * HEAD_DIM, (2 * g + 2) * HEAD_DIM)
        vcat[pl.ds(0, BLOCK), vs] = vp_ref[0, :, gs]
        vcat[pl.ds(BLOCK, tq), vs] = v_ref[0, :, gs]
        vcat[pl.ds(BLOCK + tq, BLOCK), vs] = vn_ref[0, :, gs]
        vcat[:, os_] = jnp.ones((tq + 2 * BLOCK, HEAD_DIM), BF16)
        vctx[:, vs] = vc_ref[0, :, gs]
        vctx[:, os_] = jnp.ones((clen, HEAD_DIM), BF16)

    qi = lax.broadcasted_iota(jnp.int32, (BLOCK, 3 * BLOCK), 0)
    kj = lax.broadcasted_iota(jnp.int32, (BLOCK, 3 * BLOCK), 1)
    rel = kj - qi
    band = (rel >= BLOCK - WINDOW) & (rel <= BLOCK + WINDOW)
    nt = (((1,), (1,)), ((), ()))
    rows = Q_PER_KV * BLOCK
    masks = []
    for jq in range(nqb):
        kpos = (i * nqb + jq - 1) * BLOCK + kj
        masks.append((band & (kpos >= 0) & (kpos < length))[None])

    n_loc = 3 * BLOCK

    def scores(jq, g, buf):
        gs = slice(g * HEAD_DIM, (g + 1) * HEAD_DIM)
        q4 = jnp.concatenate(
            [q_ref[0, pl.ds(jq * BLOCK, BLOCK), h * HEAD_DIM:(h + 1) * HEAD_DIM]
             for h in range(g * Q_PER_KV, (g + 1) * Q_PER_KV)], axis=0)
        s_scr[buf, :, 0:n_loc] = lax.dot_general(q4, kcat[pl.ds(jq * BLOCK, n_loc), gs], nt,
                                                 preferred_element_type=F32)
        s_scr[buf, :, n_loc:n_loc + clen] = lax.dot_general(q4, kc_ref[0, :, gs], nt, preferred_element_type=F32)

    def softmax_values(jq, g, buf):
        vos = slice(2 * g * HEAD_DIM, (2 * g + 2) * HEAD_DIM)
        heads = [g * Q_PER_KV + hh for hh in range(Q_PER_KV)]
        sink = jnp.concatenate([jnp.full((BLOCK, 1), sink_ref[h], F32) for h in heads], axis=0)
        s_loc = jnp.where(masks[jq], s_scr[buf, :, 0:n_loc].reshape(Q_PER_KV, BLOCK, n_loc), NEG_INF)
        s_loc = s_loc.reshape(rows, n_loc)
        s_ctx = s_scr[buf, :, n_loc:n_loc + clen]
        cols = ([s_loc[:, c * LANES:(c + 1) * LANES] for c in range(n_loc // LANES)]
                + [s_ctx[:, c * LANES:(c + 1) * LANES] for c in range(clen // LANES)])
        m = jnp.maximum(jnp.max(functools.reduce(jnp.maximum, cols), axis=-1, keepdims=True), sink)
        p_loc = jnp.exp(s_loc - m).astype(BF16)
        p_ctx = jnp.exp(s_ctx - m).astype(BF16)
        o2 = (jnp.dot(p_loc, vcat[pl.ds(jq * BLOCK, n_loc), vos], preferred_element_type=F32)
              + jnp.dot(p_ctx, vctx[:, vos], preferred_element_type=F32))
        denom = o2[:, HEAD_DIM:] + jnp.exp(sink - m)
        o = o2[:, :HEAD_DIM] * (1.0 / denom)
        for hh, h in enumerate(heads):
            ya_scr[pl.ds(jq * BLOCK, BLOCK), h * HEAD_DIM:(h + 1) * HEAD_DIM] = (
                o[hh * BLOCK:(hh + 1) * BLOCK].astype(ya_scr.dtype))

    order = [(jq, g) for jq in range(nqb) for g in range(N_KV_HEADS)]
    return [(functools.partial(scores, jq, g, u % 2), functools.partial(softmax_values, jq, g, u % 2))
            for u, (jq, g) in enumerate(order)]


def _mix_kernel(sink_ref, q_ref, k_ref, kp_ref, kn_ref, v_ref, vp_ref, vn_ref, kc_ref, vc_ref,
                hf_ref, hb_ref, rg_ref, ga_ref, gb_ref, x_ref, mod_ref, wr_ref, wa_ref, wo_ref, nw_ref,
                x1_ref, h2_ref, ya_scr, yr_scr, s_scr, kcat, vcat, vctx, *, nqb, length, n_steps):
    j = pl.program_id(0)
    nb = length // (nqb * BLOCK)
    slot = j % 2

    @pl.when(j == 0)
    def _():
        ya_scr[1] = jnp.zeros(ya_scr.shape[1:], ya_scr.dtype)

    units = _attention(jnp.minimum(j, n_steps - 2) % nb, sink_ref, q_ref, k_ref, kp_ref, kn_ref, v_ref, vp_ref,
                       vn_ref, kc_ref, vc_ref, ya_scr.at[slot], s_scr, kcat, vcat, vctx, nqb=nqb, length=length)

    state = {}

    def rnn_in():
        y_rnn = (hf_ref[0].astype(F32) + hb_ref[0].astype(F32)) * _gelu(rg_ref[0].astype(F32))
        yr_scr[...] = y_rnn.astype(BF16)

    def branch_piece(p, w_ref, src, key):
        cs = slice(p * MXU_DIM, (p + 1) * MXU_DIM)
        state[key, p] = jnp.dot(src(), w_ref[:, cs], preferred_element_type=F32)

    def out_piece(p):
        cs = slice(p * MXU_DIM, (p + 1) * MXU_DIM)
        y = (_sigmoid(ga_ref[0, :, cs].astype(F32)) * state.pop(("r", p))
             + _sigmoid(gb_ref[0, :, cs].astype(F32)) * state.pop(("a", p)))
        part = jnp.dot(y.astype(BF16), wo_ref[cs, :], preferred_element_type=F32)
        state["z"] = part if "z" not in state else state["z"] + part

    n_p = D_MODEL // MXU_DIM
    att = [functools.partial(branch_piece, p, wa_ref, lambda: ya_scr[1 - slot], "a") for p in range(n_p)]
    rnn = [functools.partial(branch_piece, p, wr_ref, lambda: yr_scr[...], "r") for p in range(n_p)]
    outs = [functools.partial(out_piece, p) for p in range(n_p)]
    fillers = [rnn_in] + att + rnn[:n_p - 1] + [outs[0], rnn[n_p - 1]] + outs[1:]
    per_unit = [3, 2, 2, 2, 2, 1, 1, 0]
    assert sum(per_unit) == len(fillers) and len(per_unit) == len(units)
    units[0][0]()
    for u, (_, softmax_values) in enumerate(units):
        if u + 1 < len(units):
            units[u + 1][0]()
        for f in fillers[:per_unit[u]]:
            f()
        fillers = fillers[per_unit[u]:]
        softmax_values()

    x1 = x_ref[0] + mod_ref[0, 2:3, :] * state["z"]
    x1_ref[0] = x1
    ms = jnp.mean(x1 * x1, axis=-1, keepdims=True)
    n2 = x1 * lax.rsqrt(ms + EPS) * nw_ref[...]
    h2_ref[0] = (n2 * (1.0 + mod_ref[0, 4:5, :]) + mod_ref[0, 3:4, :]).astype(h2_ref.dtype)


def _mix_call(sink, q, k, v, kc, vc, hf, hb, rg, ga, gb, x, mod, w_o_rnn, w_o_attn, w_out, nw2, nqb):
    bsz, length, _ = x.shape
    clen = kc.shape[1]
    tm = nqb * BLOCK
    n_blk = length // BLOCK
    nb = length // tm
    n_steps = bsz * nb + 1
    att = lambda j: divmod(jnp.minimum(j, n_steps - 2), nb)
    mrg = lambda j: divmod(jnp.maximum(j - 1, 0), nb)
    main = lambda j: (att(j)[0], att(j)[1], 0)
    prev = lambda j: (att(j)[0], jnp.maximum(att(j)[1] * nqb - 1, 0), 0)
    nxt = lambda j: (att(j)[0], jnp.minimum((att(j)[1] + 1) * nqb, n_blk - 1), 0)
    kv_specs = [pl.BlockSpec((1, tm, KV_WIDTH), main),
                pl.BlockSpec((1, BLOCK, KV_WIDTH), prev),
                pl.BlockSpec((1, BLOCK, KV_WIDTH), nxt)]
    ctx_spec = pl.BlockSpec((1, clen, KV_WIDTH), lambda j: (att(j)[0], 0, 0))
    blk = pl.BlockSpec((1, tm, D_MODEL), lambda j: (mrg(j)[0], mrg(j)[1], 0))
    return pl.pallas_call(
        functools.partial(_mix_kernel, nqb=nqb, length=length, n_steps=n_steps),
        grid=(n_steps,),
        in_specs=[pl.BlockSpec(memory_space=pltpu.SMEM),
                  pl.BlockSpec((1, tm, ATTN_WIDTH), main)] + kv_specs + kv_specs + [ctx_spec, ctx_spec]
                 + [blk, blk, blk, blk, blk, blk,
                    pl.BlockSpec((1, N_MOD, D_MODEL), lambda j: (mrg(j)[0], 0, 0)),
                    _const_spec((D_RNN, D_MODEL)), _const_spec((ATTN_WIDTH, D_MODEL)),
                    _const_spec((D_MODEL, D_MODEL)), _const_spec((1, D_MODEL))],
        out_specs=[blk, blk],
        out_shape=[jax.ShapeDtypeStruct((bsz, length, D_MODEL), F32),
                   jax.ShapeDtypeStruct((bsz, length, D_MODEL), BF16)],
        scratch_shapes=[pltpu.VMEM((2, tm, ATTN_WIDTH), BF16),
                        pltpu.VMEM((tm, D_RNN), BF16),
                        pltpu.VMEM((2, Q_PER_KV * BLOCK, 3 * BLOCK + clen), F32),
                        pltpu.VMEM((tm + 2 * BLOCK, KV_WIDTH), BF16),
                        pltpu.VMEM((tm + 2 * BLOCK, 2 * KV_WIDTH), BF16),
                        pltpu.VMEM((clen, 2 * KV_WIDTH), BF16)],
        compiler_params=pltpu.CompilerParams(dimension_semantics=("arbitrary",),
                                             vmem_limit_bytes=VMEM_LIMIT),
        name="mix",
    )(sink, q, k, k, k, v, v, v, kc, vc, hf, hb, rg, ga, gb, x, mod, w_o_rnn, w_o_attn, w_out, nw2)


def _ffn_kernel(h_ref, hp_ref, hn_ref, x1_ref, mod_ref, wu_ref, cw_ref, cb_ref, wd_ref, fw_ref,
                o_ref, hbuf, ua_scr, uv_scr, g_scr, *, tm, nb, n_steps):
    j = pl.program_id(0)
    i = jnp.minimum(j, n_steps - 2) % nb
    ring = j % 2

    @pl.when(j == 0)
    def _():
        g_scr[1] = jnp.zeros(g_scr.shape[1:], g_scr.dtype)

    hbuf[pl.ds(0, HALO), :] = jnp.where(i > 0, hp_ref[0], jnp.zeros_like(hp_ref[0]))
    hbuf[pl.ds(HALO, tm), :] = h_ref[0]
    hbuf[pl.ds(HALO + tm, HALO), :] = jnp.where(i < nb - 1, hn_ref[0], jnp.zeros_like(hn_ref[0]))

    rows = tm + 2 * HALO

    n_chunks = D_FF // FFN_CHUNK
    pieces = FFN_CHUNK // MXU_DIM
    lg_per_piece = MXU_DIM // LANES

    def up_piece(u_scr, c, p, c0):
        u = jnp.dot(hbuf[...], wu_ref[:, c0 + p * MXU_DIM:c0 + (p + 1) * MXU_DIM], preferred_element_type=F32)
        for q in range(lg_per_piece):
            u_scr[c % 2, p * lg_per_piece + q, pl.ds(0, rows, stride=2), :] = u[:, q * LANES:(q + 1) * LANES]

    def conv(u_scr, slot, lg, c0):
        cs = slice(c0 + lg * LANES, c0 + (lg + 1) * LANES)
        taps = [u_scr[slot, lg, pl.ds(2 * (HALO - FFN_CONV_W // 2 + k), tm, stride=2), :] for k in range(FFN_CONV_W)]
        return cb_ref[:, cs] + sum(cw_ref[k:k + 1, cs] * taps[k] for k in range(FFN_CONV_W))

    def act_piece(c, lg):
        ca, cv = c * FFN_CHUNK, D_FF + c * FFN_CHUNK
        g = _gelu(conv(ua_scr, c % 2, lg, ca)) * conv(uv_scr, c % 2, lg, cv)
        g_scr[ring, :, ca + lg * LANES:ca + (lg + 1) * LANES] = g.astype(BF16)

    half = D_MODEL // 2
    z = [None, None]

    def down_piece(c, hh):
        ks = slice(c * FFN_CHUNK, (c + 1) * FFN_CHUNK)
        part = jnp.dot(g_scr[1 - ring, :, ks], wd_ref[ks, hh * half:(hh + 1) * half], preferred_element_type=F32)
        z[hh] = part if z[hh] is None else z[hh] + part

    for p in range(pieces):
        up_piece(ua_scr, 0, p, 0)
        up_piece(uv_scr, 0, p, D_FF)
    downs = [functools.partial(down_piece, c, hh) for c in range(n_chunks) for hh in range(2)]
    n_down = [1] * (n_chunks - 1) + [len(downs) - (n_chunks - 1)]
    for c in range(n_chunks):
        mm = []
        if c + 1 < n_chunks:
            for p in range(pieces):
                mm.append(functools.partial(up_piece, ua_scr, c + 1, p, (c + 1) * FFN_CHUNK))
                mm.append(functools.partial(up_piece, uv_scr, c + 1, p, D_FF + (c + 1) * FFN_CHUNK))
        mm += downs[:n_down[c]]
        downs = downs[n_down[c]:]
        acts = [functools.partial(act_piece, c, lg) for lg in range(FFN_CHUNK // LANES)]
        while mm or acts:
            if mm:
                mm.pop(0)()
            if acts:
                acts.pop(0)()
    x2 = x1_ref[0] + mod_ref[0, 5:6, :] * jnp.concatenate(z, axis=1)
    ms = jnp.mean(x2 * x2, axis=-1, keepdims=True)
    o_ref[0] = x2 * lax.rsqrt(ms + EPS) * fw_ref[...]


def _ffn_call(h2, x1, mod, w_up, conv_w, conv_b, w_down, fw, tm):
    bsz, length, _ = x1.shape
    nb = length // tm
    n_steps = bsz * nb + 1
    hb = tm // HALO
    n_hb = length // HALO
    act = lambda j: divmod(jnp.minimum(j, n_steps - 2), nb)
    dwn = lambda j: divmod(jnp.maximum(j - 1, 0), nb)
    tok_out = lambda j: (dwn(j)[0], dwn(j)[1], 0)
    return pl.pallas_call(
        functools.partial(_ffn_kernel, tm=tm, nb=nb, n_steps=n_steps),
        grid=(n_steps,),
        in_specs=[pl.BlockSpec((1, tm, D_MODEL), lambda j: (act(j)[0], act(j)[1], 0)),
                  pl.BlockSpec((1, HALO, D_MODEL), lambda j: (act(j)[0], jnp.maximum(act(j)[1] * hb - 1, 0), 0)),
                  pl.BlockSpec((1, HALO, D_MODEL),
                               lambda j: (act(j)[0], jnp.minimum((act(j)[1] + 1) * hb, n_hb - 1), 0)),
                  pl.BlockSpec((1, tm, D_MODEL), tok_out),
                  pl.BlockSpec((1, N_MOD, D_MODEL), lambda j: (dwn(j)[0], 0, 0)),
                  _const_spec((D_MODEL, 2 * D_FF)), _const_spec((FFN_CONV_W, 2 * D_FF)),
                  _const_spec((1, 2 * D_FF)), _const_spec((D_FF, D_MODEL)), _const_spec((1, D_MODEL))],
        out_specs=pl.BlockSpec((1, tm, D_MODEL), tok_out),
        out_shape=jax.ShapeDtypeStruct((bsz, length, D_MODEL), F32),
        scratch_shapes=[pltpu.VMEM((tm + 2 * HALO, D_MODEL), BF16),
                        pltpu.VMEM((2, FFN_CHUNK // LANES, 2 * (tm + 2 * HALO), LANES), F32),
                        pltpu.VMEM((2, FFN_CHUNK // LANES, 2 * (tm + 2 * HALO), LANES), F32),
                        pltpu.VMEM((2, tm, D_FF), BF16)],
        compiler_params=pltpu.CompilerParams(dimension_semantics=("arbitrary",),
                                             vmem_limit_bytes=VMEM_LIMIT),
        name="ffn",
    )(h2, h2, h2, x1, mod, w_up, conv_w, conv_b, w_down, fw)


def _rope_tables(length):
    pos = np.arange(length)
    row, col = pos // GRID_W, pos % GRID_W
    half = HEAD_DIM // 2
    inv = ROPE_THETA ** (-np.arange(0, half, 2, dtype=np.float64) / half)
    ar, ac = row[:, None] * inv, col[:, None] * inv
    cos = np.concatenate([np.cos(ar), np.cos(ar), np.cos(ac), np.cos(ac)], axis=1)
    sin = np.concatenate([-np.sin(ar), np.sin(ar), -np.sin(ac), np.sin(ac)], axis=1)
    return jnp.asarray(cos, F32), jnp.asarray(sin, F32)


def _gate_weights(w_a, w_x):
    per = GATE_TILE // LRU_BLOCK
    eye = jnp.eye(per, dtype=F32)

    def bd(w):
        w = w.reshape(N_DIRS, GATE_TILES, per, LRU_BLOCK, LRU_BLOCK)
        return jnp.einsum('dtpcn,pq->dtpcqn', w, eye).reshape(N_DIRS, GATE_TILES, GATE_TILE, GATE_TILE)

    return jnp.concatenate([bd(w_a), bd(w_x)], axis=-1).astype(BF16)


def kernel(x, c, ctx, c_ctx, w_mod, b_mod, norm1_w, w_in, lru_conv_w, lru_conv_b, gate_a_w, gate_a_b,
           gate_x_w, gate_x_b, lru_lambda, sink_logit, w_o_rnn, w_o_attn, w_out, norm2_w, w_up,
           ffn_conv_w, ffn_conv_b, w_down, final_norm_w):
    bsz, length, _ = x.shape
    l = 0

    cs = jnp.concatenate([c, c_ctx[None], jnp.zeros((SUBLANES - bsz - 1, D_MODEL), F32)], axis=0)
    mod = _mod_call(cs, w_mod[l], b_mod[l][None]).reshape(SUBLANES, N_MOD, D_MODEL)

    w_in_b = w_in[l].astype(BF16)
    nw1 = norm1_w[l][None]
    cw, cb = lru_conv_w[l], lru_conv_b[l][None]

    plan_x = [(COL_R, D_RNN, "conv"), (COL_RG, D_RNN, "bf16"), (COL_Q, ATTN_WIDTH, "rope_q"),
              (COL_K, KV_WIDTH, "rope_k"), (COL_V, KV_WIDTH, "bf16"), (COL_GA, D_MODEL, "bf16"),
              (COL_GB, D_MODEL, "bf16")]
    xc, rg, q, k, v, ga, gb = _inproj_call(x, mod, 0, nw1, w_in_b, cw, cb, _rope_tables(length), plan_x, tm=512)
    plan_c = [(COL_R, D_RNN, "conv"), (COL_K, KV_WIDTH, "bf16"), (COL_V, KV_WIDTH, "bf16")]
    xc_ctx, kc, vc = _inproj_call(ctx, mod, bsz, nw1, w_in_b, cw, cb, None, plan_c, tm=ctx.shape[1])

    wg = _gate_weights(gate_a_w[l], gate_x_w[l])
    bg = jnp.stack([gate_a_b[l], gate_x_b[l]], axis=1)
    hf, hb = _scan_call(xc, xc_ctx, wg, bg, lru_lambda[l][:, None, :], tm=256)

    x1, h2 = _mix_call(sink_logit[l], q, k, v, kc, vc, hf, hb, rg, ga, gb, x, mod, w_o_rnn[l].astype(BF16),
                       w_o_attn[l].astype(BF16), w_out[l].astype(BF16), norm2_w[l][None], nqb=4)
    return _ffn_call(h2, x1, mod, w_up[l].astype(BF16), ffn_conv_w[l], ffn_conv_b[l][None],
                     w_down[l].astype(BF16), final_norm_w[None], tm=512)
```

```python
import functools

import numpy as np
import jax
import jax.numpy as jnp
from jax import lax
from jax.experimental import pallas as pl
from jax.experimental.pallas import tpu as pltpu

F32 = jnp.float32
BF16 = jnp.bfloat16

D_MODEL = 1024
N_HEADS = 8
N_KV_HEADS = 2
HEAD_DIM = 128
Q_PER_KV = N_HEADS // N_KV_HEADS
ATTN_WIDTH = N_HEADS * HEAD_DIM
KV_WIDTH = N_KV_HEADS * HEAD_DIM
WINDOW = 128
BLOCK = 128
GRID_W = 64
ROPE_THETA = 10000.0
D_RNN = D_MODEL
N_LRU_BLOCKS = 16
LRU_BLOCK = D_RNN // N_LRU_BLOCKS
LRU_C = 8.0
LRU_CONV_W = 4
LRU_CONV_LEFT = 2
N_DIRS = 2
D_FF = 3 * D_MODEL
FFN_CONV_W = 3
N_MOD = 6
EPS = 1e-6
NEG_INF = -1e30

COL_R = 0
COL_RG = COL_R + D_RNN
COL_Q = COL_RG + D_RNN
COL_K = COL_Q + ATTN_WIDTH
COL_V = COL_K + KV_WIDTH
COL_GA = COL_V + KV_WIDTH
COL_GB = COL_GA + D_MODEL
D_IN = COL_GB + D_MODEL

SUBLANES = 8
BF16_SUBLANES = 16
LANES = 128
MXU_DIM = 256
VMEM_LIMIT = 56 * 1024 * 1024

LANE_GROUPS = D_RNN // LANES
GATE_TILE = MXU_DIM
GATE_TILES = D_RNN // GATE_TILE
FFN_CHUNK = 512
HALO = BF16_SUBLANES
LOG2E = float(np.log2(np.e))
assert BLOCK == LANES and WINDOW >= BLOCK - 1
TINY = 1e-30


GELU_K1 = float(np.sqrt(2.0 / np.pi))
GELU_K2 = GELU_K1 * 0.044715


def _gelu(v):
    return (0.5 * v) * (1.0 + jnp.tanh(v * (GELU_K1 + GELU_K2 * (v * v))))


def _sigmoid(v):
    return 0.5 * jnp.tanh(0.5 * v) + 0.5


def _const_spec(shape):
    nd = len(shape)
    return pl.BlockSpec(shape, lambda *_: (0,) * nd, pipeline_mode=pl.Buffered(1))


def _halo_specs(tm, length, width):
    hb = tm // HALO
    n_hb = length // HALO
    prev = pl.BlockSpec((1, HALO, width), lambda b, i: (b, jnp.maximum(i * hb - 1, 0), 0))
    nxt = pl.BlockSpec((1, HALO, width), lambda b, i: (b, jnp.minimum((i + 1) * hb, n_hb - 1), 0))
    return prev, nxt


def _mod_kernel(c_ref, w_ref, b_ref, o_ref):
    c = c_ref[...]
    s = c * _sigmoid(c)
    o_ref[...] = jnp.dot(s, w_ref[...], preferred_element_type=F32) + b_ref[...]


def _mod_call(cs, w_mod, b_mod):
    n_out = w_mod.shape[1]
    tn = n_out // 4
    return pl.pallas_call(
        _mod_kernel,
        grid=(n_out // tn,),
        in_specs=[pl.BlockSpec((SUBLANES, D_MODEL), lambda j: (0, 0)),
                  pl.BlockSpec((D_MODEL, tn), lambda j: (0, j)),
                  pl.BlockSpec((1, tn), lambda j: (0, j))],
        out_specs=pl.BlockSpec((SUBLANES, tn), lambda j: (0, j)),
        out_shape=jax.ShapeDtypeStruct((SUBLANES, n_out), F32),
        compiler_params=pltpu.CompilerParams(dimension_semantics=("arbitrary",),
                                             vmem_limit_bytes=VMEM_LIMIT),
        name="mod",
    )(cs, w_mod, b_mod)


def _inproj_kernel(*refs, plan, rope, tm, nb):
    x_ref, xp_ref, xn_ref, mod_ref, nw_ref, w_ref, cw_ref, cb_ref = refs[:8]
    n_in = 10 if rope else 8
    out_refs = refs[n_in:n_in + len(plan)]
    h_scr, r_scr = refs[-2:]
    i = pl.program_id(1)

    def norm_mod(x):
        ms = jnp.mean(x * x, axis=-1, keepdims=True)
        y = x * lax.rsqrt(ms + EPS) * nw_ref[...]
        return y * (1.0 + mod_ref[0, 1:2, :]) + mod_ref[0, 0:1, :]

    h_scr[pl.ds(0, HALO), :] = jnp.where(i > 0, norm_mod(xp_ref[0]), 0.0).astype(BF16)
    h_scr[pl.ds(HALO, tm), :] = norm_mod(x_ref[0]).astype(BF16)
    h_scr[pl.ds(HALO + tm, HALO), :] = jnp.where(i < nb - 1, norm_mod(xn_ref[0]), 0.0).astype(BF16)

    if rope:
        cos, sin = refs[8][...], refs[9][...]
        lane = lax.broadcasted_iota(jnp.int32, (tm, HEAD_DIM), 1)
        low = (lane % (HEAD_DIM // 2)) < (HEAD_DIM // 4)

    def piece(c0, kind, o_ref, p):
        cs = slice(p * MXU_DIM, (p + 1) * MXU_DIM)
        wc = slice(c0 + p * MXU_DIM, c0 + (p + 1) * MXU_DIM)
        if kind == "conv":
            r_scr[:, cs] = jnp.dot(h_scr[...], w_ref[:, wc], preferred_element_type=F32)
            xc = 0.5 * cb_ref[:, cs] + sum((0.5 * cw_ref[k:k + 1, cs]) * r_scr[pl.ds(HALO - LRU_CONV_LEFT + k, tm), cs]
                                           for k in range(LRU_CONV_W))
            o_ref[0, :, cs] = xc
            return
        res = jnp.dot(h_scr[pl.ds(HALO, tm), :], w_ref[:, wc], preferred_element_type=F32)
        if kind in ("rope_q", "rope_k"):
            scale = HEAD_DIM ** -0.5 * LOG2E if kind == "rope_q" else None
            for hh in range(MXU_DIM // HEAD_DIM):
                v = res[:, hh * HEAD_DIM:(hh + 1) * HEAD_DIM]
                partner = jnp.where(low, pltpu.roll(v, HEAD_DIM - HEAD_DIM // 4, 1), pltpu.roll(v, HEAD_DIM // 4, 1))
                rot = v * cos + partner * sin
                if scale is not None:
                    rot = rot * scale
                o_ref[0, :, p * MXU_DIM + hh * HEAD_DIM:p * MXU_DIM + (hh + 1) * HEAD_DIM] = rot.astype(o_ref.dtype)
        else:
            o_ref[0, :, cs] = res.astype(o_ref.dtype)

    heavy, light = [], []
    for (c0, width, kind), o_ref in zip(plan, out_refs):
        dst = heavy if kind in ("conv", "rope_q", "rope_k") else light
        dst += [functools.partial(piece, c0, kind, o_ref, p) for p in range(width // MXU_DIM)]
    while heavy or light:
        if heavy:
            heavy.pop(0)()
        if light:
            light.pop(0)()


def _inproj_call(x, mod, mod_row0, nw, w_in, conv_w, conv_b, tables, plan, tm):
    bsz, length, _ = x.shape
    rope = tables is not None
    nb = length // tm
    tok = lambda b, i: (b, i, 0)
    prev, nxt = _halo_specs(tm, length, D_MODEL)
    in_specs = [pl.BlockSpec((1, tm, D_MODEL), tok), prev, nxt,
                pl.BlockSpec((1, N_MOD, D_MODEL), lambda b, i: (b * (mod_row0 == 0) + mod_row0, 0, 0)),
                _const_spec((1, D_MODEL)),
                _const_spec((D_MODEL, D_IN)),
                _const_spec((LRU_CONV_W, D_RNN)),
                _const_spec((1, D_RNN))]
    args = [x, x, x, mod, nw, w_in, conv_w, conv_b]
    if rope:
        in_specs += [pl.BlockSpec((tm, HEAD_DIM), lambda b, i: (i, 0))] * 2
        args += list(tables)
    out_specs, out_shape = [], []
    for (_, width, kind) in plan:
        dt = F32 if kind == "conv" else BF16
        out_specs.append(pl.BlockSpec((1, tm, width), tok))
        out_shape.append(jax.ShapeDtypeStruct((bsz, length, width), dt))
    return pl.pallas_call(
        functools.partial(_inproj_kernel, plan=tuple(plan), rope=rope, tm=tm, nb=nb),
        grid=(bsz, nb),
        in_specs=in_specs, out_specs=out_specs, out_shape=out_shape,
        scratch_shapes=[pltpu.VMEM((tm + 2 * HALO, D_MODEL), BF16),
                        pltpu.VMEM((tm + 2 * HALO, D_RNN), F32)],
        compiler_params=pltpu.CompilerParams(dimension_semantics=("parallel", "parallel"),
                                             vmem_limit_bytes=VMEM_LIMIT),
        name="inproj_x" if rope else "inproj_ctx",
    )(*args)


def _scan_kernel(xf_ref, xb_ref, xc_ref, wg_ref, bg_ref, lam_ref, hf_ref, hb_ref,
                 a3, b3, h3, hc_scr, *, tm, nb, clen, bsz):
    i = pl.program_id(0)
    chains = [(d, b) for d in range(N_DIRS) for b in range(bsz)]
    lg_per_tile = GATE_TILE // LANES

    def gate_piece(src_ref, ws, d, b, ch, kt, n):
        lam = lam_ref[d]
        softplus = jnp.maximum(-lam, 0.0) + jnp.log(1.0 + jnp.exp(-jnp.abs(lam)))
        hd2 = (-0.5 * LRU_C * LOG2E) * softplus
        cs = slice(kt * GATE_TILE, (kt + 1) * GATE_TILE)
        xk = src_ref[b, pl.ds(0, n), cs]
        g = jnp.dot(xk.astype(BF16), wg_ref[d, kt], preferred_element_type=F32)
        t_r = jnp.tanh(g[:, :GATE_TILE] + 0.5 * bg_ref[d, 0:1, cs])
        t_i = jnp.tanh(g[:, GATE_TILE:] + 0.5 * bg_ref[d, 1:2, cs])
        a = jnp.exp2(hd2[:, cs] * t_r + hd2[:, cs])
        y = 1.0 - a * a
        bb = (y * lax.rsqrt(jnp.maximum(y, TINY)) * xk) * (t_i + 1.0)
        for gg in range(lg_per_tile):
            lg = kt * lg_per_tile + gg
            ls = slice(gg * LANES, (gg + 1) * LANES)
            a3[ws, ch, pl.ds(0, n // SUBLANES), pl.ds(lg, SUBLANES, stride=LANE_GROUPS), :] = (
                a[:, ls].reshape(n // SUBLANES, SUBLANES, LANES))
            b3[ws, ch, pl.ds(0, n // SUBLANES), pl.ds(lg, SUBLANES, stride=LANE_GROUPS), :] = (
                bb[:, ls].reshape(n // SUBLANES, SUBLANES, LANES))

    def scan_group(rs, n8, tau, hs, keep):
        for rho in range(SUBLANES):
            for ch, (d, b) in enumerate(chains):
                tt, rr = (tau, rho) if d == 0 else (n8 - 1 - tau, SUBLANES - 1 - rho)
                win = pl.ds(rr * LANE_GROUPS, LANE_GROUPS)
                h = a3[rs, ch, tt, win, :] * hs[ch] + b3[rs, ch, tt, win, :]
                hs[ch] = h
                if keep:
                    h3[ch, tt, win, :] = h

    def emit(rows):
        r0, r1 = rows
        for ch, (d, b) in enumerate(chains):
            lo, hi = (r0, r1) if d == 0 else (tm - r1, tm - r0)
            o_ref = hf_ref if d == 0 else hb_ref
            for lg in range(LANE_GROUPS):
                slab = h3[ch, pl.ds(lo // SUBLANES, (hi - lo) // SUBLANES), pl.ds(lg, SUBLANES, stride=LANE_GROUPS), :]
                o_ref[b, pl.ds(lo, hi - lo), lg * LANES:(lg + 1) * LANES] = (
                    slab.reshape(hi - lo, LANES).astype(o_ref.dtype))

    @pl.when(i == 0)
    def _():
        for ch, (d, b) in enumerate(chains):
            for kt in range(GATE_TILES):
                gate_piece(xc_ref, 1, d, b, ch, kt, clen)

        def body(tau, hs):
            hs = list(hs)
            scan_group(1, clen // SUBLANES, tau, hs, False)
            return tuple(hs)

        hs = lax.fori_loop(0, clen // SUBLANES, body, (jnp.zeros((LANE_GROUPS, LANES), F32),) * len(chains))
        for ch in range(len(chains)):
            hc_scr[ch] = hs[ch]

    def step(ws, rs):
        n8 = tm // SUBLANES
        pieces = [(d, b, ch, kt) for ch, (d, b) in enumerate(chains) for kt in range(GATE_TILES)]
        per_piece = n8 // len(pieces)
        hs = [hc_scr[ch] for ch in range(len(chains))]
        tau = 0
        for (d, b, ch, kt) in pieces:
            gate_piece(xf_ref if d == 0 else xb_ref, ws, d, b, ch, kt, tm)
            for _ in range(per_piece):
                scan_group(rs, n8, tau, hs, True)
                tau += 1
                if tau == n8 // 2:
                    emit((0, tm // 2))
        emit((tm // 2, tm))
        for ch in range(len(chains)):
            hc_scr[ch] = jnp.where(i > 0, hs[ch], hc_scr[ch])

    for parity in range(2):
        pl.when(i % 2 == parity)(functools.partial(step, parity, 1 - parity))


def _scan_call(xc, xc_ctx, wg, bg, lam, tm):
    bsz, length, ch = xc.shape
    clen = xc_ctx.shape[1]
    nb = length // tm
    n_chains = N_DIRS * bsz
    assert clen == tm and (tm // SUBLANES) % (n_chains * GATE_TILES) == 0
    rows8 = tm // SUBLANES
    slab = (n_chains, rows8, LANE_GROUPS * SUBLANES, LANES)
    gi = lambda i: jnp.minimum(i, nb - 1)
    si = lambda i: jnp.maximum(i - 1, 0)
    return pl.pallas_call(
        functools.partial(_scan_kernel, tm=tm, nb=nb, clen=clen, bsz=bsz),
        grid=(nb + 1,),
        in_specs=[pl.BlockSpec((bsz, tm, ch), lambda i: (0, gi(i), 0)),
                  pl.BlockSpec((bsz, tm, ch), lambda i: (0, nb - 1 - gi(i), 0)),
                  _const_spec((bsz, clen, ch)),
                  _const_spec((N_DIRS, GATE_TILES, GATE_TILE, 2 * GATE_TILE)),
                  _const_spec((N_DIRS, 2, ch)),
                  _const_spec((N_DIRS, 1, ch))],
        out_specs=[pl.BlockSpec((bsz, tm, ch), lambda i: (0, si(i), 0)),
                   pl.BlockSpec((bsz, tm, ch), lambda i: (0, nb - 1 - si(i), 0))],
        out_shape=[jax.ShapeDtypeStruct((bsz, length, ch), BF16)] * 2,
        scratch_shapes=[pltpu.VMEM((2,) + slab, F32), pltpu.VMEM((2,) + slab, F32), pltpu.VMEM(slab, F32),
                        pltpu.VMEM((n_chains, LANE_GROUPS, LANES), F32)],
        compiler_params=pltpu.CompilerParams(dimension_semantics=("arbitrary",),
                                             vmem_limit_bytes=VMEM_LIMIT),
        name="scan",
    )(xc, xc, xc_ctx, wg, bg, lam)


def _attention(i, sink_ref, q_ref, k_ref, kp_ref, kn_ref, v_ref, vp_ref, vn_ref, kc_ref, vc_ref,
               ya_scr, s_scr, kcat, vcat, vctx, *, nqb, length):
    tq = nqb * BLOCK
    clen = kc_ref.shape[1]
    kcat[pl.ds(0, BLOCK), :] = kp_ref[0]
    kcat[pl.ds(BLOCK, tq), :] = k_ref[0]
    kcat[pl.ds(BLOCK + tq, BLOCK), :] = kn_ref[0]
    for g in range(N_KV_HEADS):
        gs = slice(g * HEAD_DIM, (g + 1) * HEAD_DIM)
        vs = slice(2 * g * HEAD_DIM, (2 * g + 1) * HEAD_DIM)
        os_ = slice((2 * g + 1) * HEAD_DIM, (2 * g + 2) * HEAD_DIM)
        vcat[pl.ds(0, BLOCK), vs] = vp_ref[0, :, gs]
        vcat[pl.ds(BLOCK, tq), vs] = v_ref[0, :, gs]
        vcat[pl.ds(BLOCK + tq, BLOCK), vs] = vn_ref[0, :, gs]
        vcat[:, os_] = jnp.ones((tq + 2 * BLOCK, HEAD_DIM), BF16)
        vctx[:, vs] = vc_ref[0, :, gs]
        vctx[:, os_] = jnp.ones((clen, HEAD_DIM), BF16)

    qi = lax.broadcasted_iota(jnp.int32, (BLOCK, 3 * BLOCK), 0)
    kj = lax.broadcasted_iota(jnp.int32, (BLOCK, 3 * BLOCK), 1)
    rel = kj - qi
    band = (rel >= BLOCK - WINDOW) & (rel <= BLOCK + WINDOW)
    nt = (((1,), (1,)), ((), ()))
    rows = Q_PER_KV * BLOCK
    masks = []
    for jq in range(nqb):
        kpos = (i * nqb + jq - 1) * BLOCK + kj
        masks.append((band & (kpos >= 0) & (kpos < length))[None])

    n_loc = 3 * BLOCK

    def scores(jq, g, buf):
        gs = slice(g * HEAD_DIM, (g + 1) * HEAD_DIM)
        q4 = jnp.concatenate(
            [q_ref[0, pl.ds(jq * BLOCK, BLOCK), h * HEAD_DIM:(h + 1) * HEAD_DIM]
             for h in range(g * Q_PER_KV, (g + 1) * Q_PER_KV)], axis=0)
        s_scr[buf, :, 0:n_loc] = lax.dot_general(q4, kcat[pl.ds(jq * BLOCK, n_loc), gs], nt,
                                                 preferred_element_type=F32)
        s_scr[buf, :, n_loc:n_loc + clen] = lax.dot_general(q4, kc_ref[0, :, gs], nt, preferred_element_type=F32)

    def softmax_values(jq, g, buf):
        vos = slice(2 * g * HEAD_DIM, (2 * g + 2) * HEAD_DIM)
        heads = [g * Q_PER_KV + hh for hh in range(Q_PER_KV)]
        sink = jnp.concatenate([jnp.full((BLOCK, 1), sink_ref[h] * LOG2E, F32) for h in heads], axis=0)
        cols = []
        for c in range(n_loc // BLOCK):
            s = s_scr[buf, :, c * BLOCK:(c + 1) * BLOCK]
            if c != 1:
                s = jnp.where(masks[jq][:, :, c * BLOCK:(c + 1) * BLOCK], s.reshape(Q_PER_KV, BLOCK, BLOCK),
                              NEG_INF).reshape(rows, BLOCK)
            cols.append(s)
        cols += [s_scr[buf, :, n_loc + c * LANES:n_loc + (c + 1) * LANES] for c in range(clen // LANES)]
        m = jnp.maximum(jnp.max(functools.reduce(jnp.maximum, cols), axis=-1, keepdims=True), sink)
        p = [jnp.exp2(s - m).astype(BF16) for s in cols]
        p_loc = jnp.concatenate(p[:n_loc // BLOCK], axis=1)
        p_ctx = jnp.concatenate(p[n_loc // BLOCK:], axis=1)
        o2 = (jnp.dot(p_loc, vcat[pl.ds(jq * BLOCK, n_loc), vos], preferred_element_type=F32)
              + jnp.dot(p_ctx, vctx[:, vos], preferred_element_type=F32))
        denom = o2[:, HEAD_DIM:] + jnp.exp2(sink - m)
        o = o2[:, :HEAD_DIM] * (1.0 / denom)
        for hh, h in enumerate(heads):
            ya_scr[pl.ds(jq * BLOCK, BLOCK), h * HEAD_DIM:(h + 1) * HEAD_DIM] = (
                o[hh * BLOCK:(hh + 1) * BLOCK].astype(ya_scr.dtype))

    order = [(jq, g) for jq in range(nqb) for g in range(N_KV_HEADS)]
    return [(functools.partial(scores, jq, g, u % 2), functools.partial(softmax_values, jq, g, u % 2))
            for u, (jq, g) in enumerate(order)]


def _mix_kernel(sink_ref, q_ref, k_ref, kp_ref, kn_ref, v_ref, vp_ref, vn_ref, kc_ref, vc_ref,
                hf_ref, hb_ref, rg_ref, ga_ref, gb_ref, x_ref, mod_ref, wr_ref, wa_ref, wo_ref, nw_ref,
                x1_ref, h2_ref, ya_scr, yr_scr, s_scr, kcat, vcat, vctx, *, nqb, length, n_steps):
    j = pl.program_id(0)
    nb = length // (nqb * BLOCK)
    slot = j % 2

    @pl.when(j == 0)
    def _():
        ya_scr[1] = jnp.zeros(ya_scr.shape[1:], ya_scr.dtype)

    units = _attention(jnp.minimum(j, n_steps - 2) % nb, sink_ref, q_ref, k_ref, kp_ref, kn_ref, v_ref, vp_ref,
                       vn_ref, kc_ref, vc_ref, ya_scr.at[slot], s_scr, kcat, vcat, vctx, nqb=nqb, length=length)

    state = {}

    def rnn_in():
        y_rnn = (hf_ref[0].astype(F32) + hb_ref[0].astype(F32)) * _gelu(rg_ref[0].astype(F32))
        yr_scr[...] = y_rnn.astype(BF16)

    def branch_piece(p, w_ref, src, key):
        cs = slice(p * MXU_DIM, (p + 1) * MXU_DIM)
        state[key, p] = jnp.dot(src(), w_ref[:, cs], preferred_element_type=F32)

    def out_piece(p):
        cs = slice(p * MXU_DIM, (p + 1) * MXU_DIM)
        y = (_sigmoid(ga_ref[0, :, cs].astype(F32)) * state.pop(("r", p))
             + _sigmoid(gb_ref[0, :, cs].astype(F32)) * state.pop(("a", p)))
        part = jnp.dot(y.astype(BF16), wo_ref[cs, :], preferred_element_type=F32)
        state["z"] = part if "z" not in state else state["z"] + part

    n_p = D_MODEL // MXU_DIM
    att = [functools.partial(branch_piece, p, wa_ref, lambda: ya_scr[1 - slot], "a") for p in range(n_p)]
    rnn = [functools.partial(branch_piece, p, wr_ref, lambda: yr_scr[...], "r") for p in range(n_p)]
    outs = [functools.partial(out_piece, p) for p in range(n_p)]
    fillers = [rnn_in] + att + rnn[:n_p - 1] + [outs[0], rnn[n_p - 1]] + outs[1:]
    per_unit = [2, 2, 1, 2, 1, 2, 1, 2]
    assert sum(per_unit) == len(fillers) and len(per_unit) == len(units)
    units[0][0]()
    for u, (_, softmax_values) in enumerate(units):
        if u + 1 < len(units):
            units[u + 1][0]()
        for f in fillers[:per_unit[u]]:
            f()
        fillers = fillers[per_unit[u]:]
        softmax_values()

    x1 = x_ref[0] + mod_ref[0, 2:3, :] * state["z"]
    x1_ref[0] = x1
    ms = jnp.mean(x1 * x1, axis=-1, keepdims=True)
    n2 = x1 * lax.rsqrt(ms + EPS) * nw_ref[...]
    h2_ref[0] = (n2 * (1.0 + mod_ref[0, 4:5, :]) + mod_ref[0, 3:4, :]).astype(h2_ref.dtype)


def _mix_call(sink, q, k, v, kc, vc, hf, hb, rg, ga, gb, x, mod, w_o_rnn, w_o_attn, w_out, nw2, nqb):
    bsz, length, _ = x.shape
    clen = kc.shape[1]
    tm = nqb * BLOCK
    n_blk = length // BLOCK
    nb = length // tm
    n_steps = bsz * nb + 1
    att = lambda j: divmod(jnp.minimum(j, n_steps - 2), nb)
    mrg = lambda j: divmod(jnp.maximum(j - 1, 0), nb)
    main = lambda j: (att(j)[0], att(j)[1], 0)
    prev = lambda j: (att(j)[0], jnp.maximum(att(j)[1] * nqb - 1, 0), 0)
    nxt = lambda j: (att(j)[0], jnp.minimum((att(j)[1] + 1) * nqb, n_blk - 1), 0)
    kv_specs = [pl.BlockSpec((1, tm, KV_WIDTH), main),
                pl.BlockSpec((1, BLOCK, KV_WIDTH), prev),
                pl.BlockSpec((1, BLOCK, KV_WIDTH), nxt)]
    ctx_spec = pl.BlockSpec((1, clen, KV_WIDTH), lambda j: (att(j)[0], 0, 0))
    blk = pl.BlockSpec((1, tm, D_MODEL), lambda j: (mrg(j)[0], mrg(j)[1], 0))
    return pl.pallas_call(
        functools.partial(_mix_kernel, nqb=nqb, length=length, n_steps=n_steps),
        grid=(n_steps,),
        in_specs=[pl.BlockSpec(memory_space=pltpu.SMEM),
                  pl.BlockSpec((1, tm, ATTN_WIDTH), main)] + kv_specs + kv_specs + [ctx_spec, ctx_spec]
                 + [blk, blk, blk, blk, blk, blk,
                    pl.BlockSpec((1, N_MOD, D_MODEL), lambda j: (mrg(j)[0], 0, 0)),
                    _const_spec((D_RNN, D_MODEL)), _const_spec((ATTN_WIDTH, D_MODEL)),
                    _const_spec((D_MODEL, D_MODEL)), _const_spec((1, D_MODEL))],
        out_specs=[blk, blk],
        out_shape=[jax.ShapeDtypeStruct((bsz, length, D_MODEL), F32),
                   jax.ShapeDtypeStruct((bsz, length, D_MODEL), BF16)],
        scratch_shapes=[pltpu.VMEM((2, tm, ATTN_WIDTH), BF16),
                        pltpu.VMEM((tm, D_RNN), BF16),
                        pltpu.VMEM((2, Q_PER_KV * BLOCK, 3 * BLOCK + clen), F32),
                        pltpu.VMEM((tm + 2 * BLOCK, KV_WIDTH), BF16),
                        pltpu.VMEM((tm + 2 * BLOCK, 2 * KV_WIDTH), BF16),
                        pltpu.VMEM((clen, 2 * KV_WIDTH), BF16)],
        compiler_params=pltpu.CompilerParams(dimension_semantics=("arbitrary",),
                                             vmem_limit_bytes=VMEM_LIMIT),
        name="mix",
    )(sink, q, k, k, k, v, v, v, kc, vc, hf, hb, rg, ga, gb, x, mod, w_o_rnn, w_o_attn, w_out, nw2)


def _ffn_kernel(h_ref, hp_ref, hn_ref, x1_ref, mod_ref, wu_ref, cw_ref, cb_ref, wd_ref, fw_ref,
                o_ref, hbuf, ua_scr, uv_scr, g_scr, *, tm, nb, n_steps):
    j = pl.program_id(0)
    i = jnp.minimum(j, n_steps - 2) % nb
    ring = j % 2

    @pl.when(j == 0)
    def _():
        g_scr[1] = jnp.zeros(g_scr.shape[1:], g_scr.dtype)

    hbuf[pl.ds(0, HALO), :] = jnp.where(i > 0, hp_ref[0], jnp.zeros_like(hp_ref[0]))
    hbuf[pl.ds(HALO, tm), :] = h_ref[0]
    hbuf[pl.ds(HALO + tm, HALO), :] = jnp.where(i < nb - 1, hn_ref[0], jnp.zeros_like(hn_ref[0]))

    rows = tm + 2 * HALO

    n_chunks = D_FF // FFN_CHUNK
    pieces = FFN_CHUNK // MXU_DIM
    lg_per_piece = MXU_DIM // LANES

    def up_piece(u_scr, c, p, c0):
        u = jnp.dot(hbuf[...], wu_ref[:, c0 + p * MXU_DIM:c0 + (p + 1) * MXU_DIM], preferred_element_type=F32)
        for q in range(lg_per_piece):
            u_scr[c % 2, p * lg_per_piece + q, pl.ds(0, rows, stride=2), :] = u[:, q * LANES:(q + 1) * LANES]

    def conv(u_scr, slot, lg, c0):
        cs = slice(c0 + lg * LANES, c0 + (lg + 1) * LANES)
        taps = [u_scr[slot, lg, pl.ds(2 * (HALO - FFN_CONV_W // 2 + k), tm, stride=2), :] for k in range(FFN_CONV_W)]
        return cb_ref[:, cs] + sum(cw_ref[k:k + 1, cs] * taps[k] for k in range(FFN_CONV_W))

    def act_piece(c, lg):
        ca, cv = c * FFN_CHUNK, D_FF + c * FFN_CHUNK
        g = _gelu(conv(ua_scr, c % 2, lg, ca)) * conv(uv_scr, c % 2, lg, cv)
        g_scr[ring, :, ca + lg * LANES:ca + (lg + 1) * LANES] = g.astype(BF16)

    half = D_MODEL // 2
    z = [None, None]

    def down_piece(c, hh):
        ks = slice(c * FFN_CHUNK, (c + 1) * FFN_CHUNK)
        part = jnp.dot(g_scr[1 - ring, :, ks], wd_ref[ks, hh * half:(hh + 1) * half], preferred_element_type=F32)
        z[hh] = part if z[hh] is None else z[hh] + part

    for p in range(pieces):
        up_piece(ua_scr, 0, p, 0)
        up_piece(uv_scr, 0, p, D_FF)
    downs = [functools.partial(down_piece, c, hh) for c in range(n_chunks) for hh in range(2)]
    n_down = [1] * (n_chunks - 1) + [len(downs) - (n_chunks - 1)]
    for c in range(n_chunks):
        mm = []
        if c + 1 < n_chunks:
            for p in range(pieces):
                mm.append(functools.partial(up_piece, ua_scr, c + 1, p, (c + 1) * FFN_CHUNK))
                mm.append(functools.partial(up_piece, uv_scr, c + 1, p, D_FF + (c + 1) * FFN_CHUNK))
        mm += downs[:n_down[c]]
        downs = downs[n_down[c]:]
        acts = [functools.partial(act_piece, c, lg) for lg in range(FFN_CHUNK // LANES)]
        while mm or acts:
            if mm:
                mm.pop(0)()
            if acts:
                acts.pop(0)()
    x2 = x1_ref[0] + mod_ref[0, 5:6, :] * jnp.concatenate(z, axis=1)
    ms = jnp.mean(x2 * x2, axis=-1, keepdims=True)
    o_ref[0] = x2 * lax.rsqrt(ms + EPS) * fw_ref[...]


def _ffn_call(h2, x1, mod, w_up, conv_w, conv_b, w_down, fw, tm):
    bsz, length, _ = x1.shape
    nb = length // tm
    n_steps = bsz * nb + 1
    hb = tm // HALO
    n_hb = length // HALO
    act = lambda j: divmod(jnp.minimum(j, n_steps - 2), nb)
    dwn = lambda j: divmod(jnp.maximum(j - 1, 0), nb)
    tok_out = lambda j: (dwn(j)[0], dwn(j)[1], 0)
    return pl.pallas_call(
        functools.partial(_ffn_kernel, tm=tm, nb=nb, n_steps=n_steps),
        grid=(n_steps,),
        in_specs=[pl.BlockSpec((1, tm, D_MODEL), lambda j: (act(j)[0], act(j)[1], 0)),
                  pl.BlockSpec((1, HALO, D_MODEL), lambda j: (act(j)[0], jnp.maximum(act(j)[1] * hb - 1, 0), 0)),
                  pl.BlockSpec((1, HALO, D_MODEL),
                               lambda j: (act(j)[0], jnp.minimum((act(j)[1] + 1) * hb, n_hb - 1), 0)),
                  pl.BlockSpec((1, tm, D_MODEL), tok_out),
                  pl.BlockSpec((1, N_MOD, D_MODEL), lambda j: (dwn(j)[0], 0, 0)),
                  _const_spec((D_MODEL, 2 * D_FF)), _const_spec((FFN_CONV_W, 2 * D_FF)),
                  _const_spec((1, 2 * D_FF)), _const_spec((D_FF, D_MODEL)), _const_spec((1, D_MODEL))],
        out_specs=pl.BlockSpec((1, tm, D_MODEL), tok_out),
        out_shape=jax.ShapeDtypeStruct((bsz, length, D_MODEL), F32),
        scratch_shapes=[pltpu.VMEM((tm + 2 * HALO, D_MODEL), BF16),
                        pltpu.VMEM((2, FFN_CHUNK // LANES, 2 * (tm + 2 * HALO), LANES), F32),
                        pltpu.VMEM((2, FFN_CHUNK // LANES, 2 * (tm + 2 * HALO), LANES), F32),
                        pltpu.VMEM((2, tm, D_FF), BF16)],
        compiler_params=pltpu.CompilerParams(dimension_semantics=("arbitrary",),
                                             vmem_limit_bytes=VMEM_LIMIT),
        name="ffn",
    )(h2, h2, h2, x1, mod, w_up, conv_w, conv_b, w_down, fw)


def _rope_tables(length):
    pos = np.arange(length)
    row, col = pos // GRID_W, pos % GRID_W
    half = HEAD_DIM // 2
    inv = ROPE_THETA ** (-np.arange(0, half, 2, dtype=np.float64) / half)
    ar, ac = row[:, None] * inv, col[:, None] * inv
    cos = np.concatenate([np.cos(ar), np.cos(ar), np.cos(ac), np.cos(ac)], axis=1)
    sin = np.concatenate([-np.sin(ar), np.sin(ar), -np.sin(ac), np.sin(ac)], axis=1)
    return jnp.asarray(cos, F32), jnp.asarray(sin, F32)


def _gate_weights(w_a, w_x):
    per = GATE_TILE // LRU_BLOCK
    eye = jnp.eye(per, dtype=F32)

    def bd(w):
        w = w.reshape(N_DIRS, GATE_TILES, per, LRU_BLOCK, LRU_BLOCK)
        return jnp.einsum('dtpcn,pq->dtpcqn', w, eye).reshape(N_DIRS, GATE_TILES, GATE_TILE, GATE_TILE)

    return jnp.concatenate([bd(w_a), bd(w_x)], axis=-1).astype(BF16)


def kernel(x, c, ctx, c_ctx, w_mod, b_mod, norm1_w, w_in, lru_conv_w, lru_conv_b, gate_a_w, gate_a_b,
           gate_x_w, gate_x_b, lru_lambda, sink_logit, w_o_rnn, w_o_attn, w_out, norm2_w, w_up,
           ffn_conv_w, ffn_conv_b, w_down, final_norm_w):
    bsz, length, _ = x.shape
    l = 0

    cs = jnp.concatenate([c, c_ctx[None], jnp.zeros((SUBLANES - bsz - 1, D_MODEL), F32)], axis=0)
    mod = _mod_call(cs, w_mod[l], b_mod[l][None]).reshape(SUBLANES, N_MOD, D_MODEL)

    w_in_b = w_in[l].astype(BF16)
    nw1 = norm1_w[l][None]
    cw, cb = lru_conv_w[l], lru_conv_b[l][None]

    plan_x = [(COL_R, D_RNN, "conv"), (COL_RG, D_RNN, "bf16"), (COL_Q, ATTN_WIDTH, "rope_q"),
              (COL_K, KV_WIDTH, "rope_k"), (COL_V, KV_WIDTH, "bf16"), (COL_GA, D_MODEL, "bf16"),
              (COL_GB, D_MODEL, "bf16")]
    xc, rg, q, k, v, ga, gb = _inproj_call(x, mod, 0, nw1, w_in_b, cw, cb, _rope_tables(length), plan_x, tm=512)
    plan_c = [(COL_R, D_RNN, "conv"), (COL_K, KV_WIDTH, "bf16"), (COL_V, KV_WIDTH, "bf16")]
    xc_ctx, kc, vc = _inproj_call(ctx, mod, bsz, nw1, w_in_b, cw, cb, None, plan_c, tm=ctx.shape[1])

    wg = _gate_weights(gate_a_w[l], gate_x_w[l])
    bg = jnp.stack([gate_a_b[l], gate_x_b[l]], axis=1)
    hf, hb = _scan_call(xc, xc_ctx, wg, bg, lru_lambda[l][:, None, :], tm=256)

    x1, h2 = _mix_call(sink_logit[l], q, k, v, kc, vc, hf, hb, rg, ga, gb, x, mod, w_o_rnn[l].astype(BF16),
                       w_o_attn[l].astype(BF16), w_out[l].astype(BF16), norm2_w[l][None], nqb=4)
    return _ffn_call(h2, x1, mod, w_up[l].astype(BF16), ffn_conv_w[l], ffn_conv_b[l][None],
                     w_down[l].astype(BF16), final_norm_w[None], tm=512)
```

```python
import functools

import numpy as np
import jax
import jax.numpy as jnp
from jax import lax
from jax.experimental import pallas as pl
from jax.experimental.pallas import tpu as pltpu

F32 = jnp.float32
BF16 = jnp.bfloat16

D_MODEL = 1024
N_HEADS = 8
N_KV_HEADS = 2
HEAD_DIM = 128
Q_PER_KV = N_HEADS // N_KV_HEADS
ATTN_WIDTH = N_HEADS * HEAD_DIM
KV_WIDTH = N_KV_HEADS * HEAD_DIM
WINDOW = 128
BLOCK = 128
GRID_W = 64
ROPE_THETA = 10000.0
D_RNN = D_MODEL
N_LRU_BLOCKS = 16
LRU_BLOCK = D_RNN // N_LRU_BLOCKS
LRU_C = 8.0
LRU_CONV_W = 4
LRU_CONV_LEFT = 2
N_DIRS = 2
D_FF = 3 * D_MODEL
FFN_CONV_W = 3
N_MOD = 6
EPS = 1e-6
NEG_INF = -1e30

COL_R = 0
COL_RG = COL_R + D_RNN
COL_Q = COL_RG + D_RNN
COL_K = COL_Q + ATTN_WIDTH
COL_V = COL_K + KV_WIDTH
COL_GA = COL_V + KV_WIDTH
COL_GB = COL_GA + D_MODEL
D_IN = COL_GB + D_MODEL

SUBLANES = 8
BF16_SUBLANES = 16
LANES = 128
MXU_DIM = 256
VMEM_LIMIT = 56 * 1024 * 1024

LANE_GROUPS = D_RNN // LANES
GATE_TILE = MXU_DIM
GATE_TILES = D_RNN // GATE_TILE
FFN_CHUNK = 512
HALO = BF16_SUBLANES
LOG2E = float(np.log2(np.e))
assert BLOCK == LANES and WINDOW >= BLOCK - 1
TINY = 1e-30


GELU_K1 = float(np.sqrt(2.0 / np.pi))
GELU_K2 = GELU_K1 * 0.044715


def _gelu(v):
    return (0.5 * v) * (1.0 + jnp.tanh(v * (GELU_K1 + GELU_K2 * (v * v))))


def _sigmoid(v):
    return 0.5 * jnp.tanh(0.5 * v) + 0.5


def _const_spec(shape):
    nd = len(shape)
    return pl.BlockSpec(shape, lambda *_: (0,) * nd, pipeline_mode=pl.Buffered(1))


def _halo_specs(tm, length, width):
    hb = tm // HALO
    n_hb = length // HALO
    prev = pl.BlockSpec((1, HALO, width), lambda b, i: (b, jnp.maximum(i * hb - 1, 0), 0))
    nxt = pl.BlockSpec((1, HALO, width), lambda b, i: (b, jnp.minimum((i + 1) * hb, n_hb - 1), 0))
    return prev, nxt


def _mod_kernel(c_ref, w_ref, b_ref, o_ref):
    c = c_ref[...]
    s = c * _sigmoid(c)
    o_ref[...] = jnp.dot(s, w_ref[...], preferred_element_type=F32) + b_ref[...]


def _mod_call(cs, w_mod, b_mod):
    n_out = w_mod.shape[1]
    tn = n_out // 4
    return pl.pallas_call(
        _mod_kernel,
        grid=(n_out // tn,),
        in_specs=[pl.BlockSpec((SUBLANES, D_MODEL), lambda j: (0, 0)),
                  pl.BlockSpec((D_MODEL, tn), lambda j: (0, j)),
                  pl.BlockSpec((1, tn), lambda j: (0, j))],
        out_specs=pl.BlockSpec((SUBLANES, tn), lambda j: (0, j)),
        out_shape=jax.ShapeDtypeStruct((SUBLANES, n_out), F32),
        compiler_params=pltpu.CompilerParams(dimension_semantics=("arbitrary",),
                                             vmem_limit_bytes=VMEM_LIMIT),
        name="mod",
    )(cs, w_mod, b_mod)


def _inproj_kernel(*refs, plan, rope, tm, nb, n_steps):
    x_ref, xp_ref, xn_ref, mod_ref, nw_ref, w_ref, cw_ref, cb_ref = refs[:8]
    n_in = 10 if rope else 8
    out_refs = refs[n_in:n_in + len(plan)]
    h_scr, r_scr = refs[-2:]
    j = pl.program_id(0)
    i = jnp.minimum(j, n_steps - 2) % nb
    slot = j % 2
    h_in = h_scr.at[1 - slot]

    def norm_mod(x):
        ms = jnp.mean(x * x, axis=-1, keepdims=True)
        y = x * lax.rsqrt(ms + EPS) * nw_ref[...]
        return y * (1.0 + mod_ref[0, 1:2, :]) + mod_ref[0, 0:1, :]

    def norm_stage():
        h_scr[slot, pl.ds(0, HALO), :] = jnp.where(i > 0, norm_mod(xp_ref[0]), 0.0).astype(BF16)
        h_scr[slot, pl.ds(HALO, tm), :] = norm_mod(x_ref[0]).astype(BF16)
        h_scr[slot, pl.ds(HALO + tm, HALO), :] = jnp.where(i < nb - 1, norm_mod(xn_ref[0]), 0.0).astype(BF16)

    def piece(c0, kind, o_ref, p):
        cs = slice(p * MXU_DIM, (p + 1) * MXU_DIM)
        wc = slice(c0 + p * MXU_DIM, c0 + (p + 1) * MXU_DIM)
        if kind == "conv":
            r = jnp.dot(h_in[...], w_ref[:, wc], preferred_element_type=F32)
            for q in range(MXU_DIM // LANES):
                lg = p * (MXU_DIM // LANES) + q
                ls = slice(lg * LANES, (lg + 1) * LANES)
                r_scr[lg, pl.ds(0, tm + 2 * HALO, stride=2), :] = r[:, q * LANES:(q + 1) * LANES]
                taps = [r_scr[lg, pl.ds(2 * (HALO - LRU_CONV_LEFT + k), tm, stride=2), :] for k in range(LRU_CONV_W)]
                o_ref[0, :, ls] = 0.5 * cb_ref[:, ls] + sum((0.5 * cw_ref[k:k + 1, ls]) * taps[k]
                                                            for k in range(LRU_CONV_W))
            return
        res = jnp.dot(h_in[pl.ds(HALO, tm), :], w_ref[:, wc], preferred_element_type=F32)
        if kind in ("rope_q", "rope_k"):
            scale = HEAD_DIM ** -0.5 * LOG2E if kind == "rope_q" else None
            cos, sin = refs[8][...], refs[9][...]
            lane = lax.broadcasted_iota(jnp.int32, (tm, HEAD_DIM), 1)
            low = (lane % (HEAD_DIM // 2)) < (HEAD_DIM // 4)
            for hh in range(MXU_DIM // HEAD_DIM):
                v = res[:, hh * HEAD_DIM:(hh + 1) * HEAD_DIM]
                partner = jnp.where(low, pltpu.roll(v, HEAD_DIM - HEAD_DIM // 4, 1), pltpu.roll(v, HEAD_DIM // 4, 1))
                rot = v * cos + partner * sin
                if scale is not None:
                    rot = rot * scale
                o_ref[0, :, p * MXU_DIM + hh * HEAD_DIM:p * MXU_DIM + (hh + 1) * HEAD_DIM] = rot.astype(o_ref.dtype)
        else:
            act = {"gelu": _gelu, "sigmoid": _sigmoid, "bf16": lambda t: t}[kind]
            o_ref[0, :, cs] = act(res).astype(o_ref.dtype)

    def project():
        by_kind = {}
        for (c0, width, kind), o_ref in zip(plan, out_refs):
            by_kind.setdefault(kind, []).extend(
                functools.partial(piece, c0, kind, o_ref, p) for p in range(width // MXU_DIM))
        take = lambda kind: by_kind[kind].pop(0)() if by_kind.get(kind) else None
        take("conv"), take("sigmoid")
        norm_stage()
        take("sigmoid"), take("bf16")
        while by_kind.get("conv"):
            take("conv"), take("sigmoid"), take("sigmoid"), take("bf16")
        while by_kind.get("rope_q") or by_kind.get("rope_k"):
            take("rope_q"), take("gelu"), take("rope_k"), take("bf16")
        for kind in list(by_kind):
            while by_kind[kind]:
                take(kind)

    pl.when(j == 0)(norm_stage)
    pl.when(j > 0)(project)


def _inproj_call(x, mod, mod_row0, nw, w_in, conv_w, conv_b, tables, plan, tm):
    bsz, length, _ = x.shape
    rope = tables is not None
    nb = length // tm
    n_steps = bsz * nb + 1
    hb = tm // HALO
    n_hb = length // HALO
    nrm = lambda j: divmod(jnp.minimum(j, n_steps - 2), nb)
    prj = lambda j: divmod(jnp.maximum(j - 1, 0), nb)
    tok_out = lambda j: (prj(j)[0], prj(j)[1], 0)
    in_specs = [pl.BlockSpec((1, tm, D_MODEL), lambda j: (nrm(j)[0], nrm(j)[1], 0)),
                pl.BlockSpec((1, HALO, D_MODEL), lambda j: (nrm(j)[0], jnp.maximum(nrm(j)[1] * hb - 1, 0), 0)),
                pl.BlockSpec((1, HALO, D_MODEL),
                             lambda j: (nrm(j)[0], jnp.minimum((nrm(j)[1] + 1) * hb, n_hb - 1), 0)),
                pl.BlockSpec((1, N_MOD, D_MODEL), lambda j: (nrm(j)[0] * (mod_row0 == 0) + mod_row0, 0, 0)),
                _const_spec((1, D_MODEL)),
                _const_spec((D_MODEL, D_IN)),
                _const_spec((LRU_CONV_W, D_RNN)),
                _const_spec((1, D_RNN))]
    args = [x, x, x, mod, nw, w_in, conv_w, conv_b]
    if rope:
        in_specs += [pl.BlockSpec((tm, HEAD_DIM), lambda j: (prj(j)[1], 0))] * 2
        args += list(tables)
    out_specs, out_shape = [], []
    for (_, width, kind) in plan:
        dt = F32 if kind == "conv" else BF16
        out_specs.append(pl.BlockSpec((1, tm, width), tok_out))
        out_shape.append(jax.ShapeDtypeStruct((bsz, length, width), dt))
    return pl.pallas_call(
        functools.partial(_inproj_kernel, plan=tuple(plan), rope=rope, tm=tm, nb=nb, n_steps=n_steps),
        grid=(n_steps,),
        in_specs=in_specs, out_specs=out_specs, out_shape=out_shape,
        scratch_shapes=[pltpu.VMEM((2, tm + 2 * HALO, D_MODEL), BF16),
                        pltpu.VMEM((D_RNN // LANES, 2 * (tm + 2 * HALO), LANES), F32)],
        compiler_params=pltpu.CompilerParams(dimension_semantics=("arbitrary",),
                                             vmem_limit_bytes=VMEM_LIMIT),
        name="inproj_x" if rope else "inproj_ctx",
    )(*args)


def _scan_kernel(xf_ref, xb_ref, xc_ref, wg_ref, bg_ref, lam_ref, hf_ref, hb_ref,
                 a3, b3, h3, hc_scr, *, tm, nb, clen, bsz):
    i = pl.program_id(0)
    chains = [(d, b) for d in range(N_DIRS) for b in range(bsz)]
    lg_per_tile = GATE_TILE // LANES

    def gate_piece(src_ref, ws, d, b, ch, kt, n):
        lam = lam_ref[d]
        softplus = jnp.maximum(-lam, 0.0) + jnp.log(1.0 + jnp.exp(-jnp.abs(lam)))
        hd2 = (-0.5 * LRU_C * LOG2E) * softplus
        cs = slice(kt * GATE_TILE, (kt + 1) * GATE_TILE)
        xk = src_ref[b, pl.ds(0, n), cs]
        g = jnp.dot(xk.astype(BF16), wg_ref[d, kt], preferred_element_type=F32)
        t_r = jnp.tanh(g[:, :GATE_TILE] + 0.5 * bg_ref[d, 0:1, cs])
        t_i = jnp.tanh(g[:, GATE_TILE:] + 0.5 * bg_ref[d, 1:2, cs])
        a = jnp.exp2(hd2[:, cs] * t_r + hd2[:, cs])
        y = 1.0 - a * a
        bb = (y * lax.rsqrt(jnp.maximum(y, TINY)) * xk) * (t_i + 1.0)
        for gg in range(lg_per_tile):
            lg = kt * lg_per_tile + gg
            ls = slice(gg * LANES, (gg + 1) * LANES)
            a3[ws, ch, pl.ds(0, n // SUBLANES), pl.ds(lg, SUBLANES, stride=LANE_GROUPS), :] = (
                a[:, ls].reshape(n // SUBLANES, SUBLANES, LANES))
            b3[ws, ch, pl.ds(0, n // SUBLANES), pl.ds(lg, SUBLANES, stride=LANE_GROUPS), :] = (
                bb[:, ls].reshape(n // SUBLANES, SUBLANES, LANES))

    def scan_group(rs, n8, tau, hs, keep):
        for rho in range(SUBLANES):
            for ch, (d, b) in enumerate(chains):
                tt, rr = (tau, rho) if d == 0 else (n8 - 1 - tau, SUBLANES - 1 - rho)
                win = pl.ds(rr * LANE_GROUPS, LANE_GROUPS)
                h = a3[rs, ch, tt, win, :] * hs[ch] + b3[rs, ch, tt, win, :]
                hs[ch] = h
                if keep:
                    h3[ch, tt, win, :] = h

    def emit(rows):
        r0, r1 = rows
        for ch, (d, b) in enumerate(chains):
            lo, hi = (r0, r1) if d == 0 else (tm - r1, tm - r0)
            o_ref = hf_ref if d == 0 else hb_ref
            for lg in range(LANE_GROUPS):
                slab = h3[ch, pl.ds(lo // SUBLANES, (hi - lo) // SUBLANES), pl.ds(lg, SUBLANES, stride=LANE_GROUPS), :]
                o_ref[b, pl.ds(lo, hi - lo), lg * LANES:(lg + 1) * LANES] = (
                    slab.reshape(hi - lo, LANES).astype(o_ref.dtype))

    @pl.when(i == 0)
    def _():
        for ch, (d, b) in enumerate(chains):
            for kt in range(GATE_TILES):
                gate_piece(xc_ref, 1, d, b, ch, kt, clen)

        def body(tau, hs):
            hs = list(hs)
            scan_group(1, clen // SUBLANES, tau, hs, False)
            return tuple(hs)

        hs = lax.fori_loop(0, clen // SUBLANES, body, (jnp.zeros((LANE_GROUPS, LANES), F32),) * len(chains))
        for ch in range(len(chains)):
            hc_scr[ch] = hs[ch]

    def step(ws, rs, gate, scan):
        n8 = tm // SUBLANES
        pieces = [(d, b, ch, kt) for ch, (d, b) in enumerate(chains) for kt in range(GATE_TILES)]
        per_piece = n8 // len(pieces)
        hs = [hc_scr[ch] for ch in range(len(chains))] if scan else None
        tau = 0
        for (d, b, ch, kt) in pieces:
            if gate:
                gate_piece(xf_ref if d == 0 else xb_ref, ws, d, b, ch, kt, tm)
            for _ in range(per_piece if scan else 0):
                scan_group(rs, n8, tau, hs, True)
                tau += 1
                if tau == n8 // 2:
                    emit((0, tm // 2))
        if scan:
            emit((tm // 2, tm))
            for ch in range(len(chains)):
                hc_scr[ch] = hs[ch]

    pl.when(i == 0)(functools.partial(step, 0, 1, True, False))
    for parity in range(2):
        pl.when((i > 0) & (i < nb) & (i % 2 == parity))(functools.partial(step, parity, 1 - parity, True, True))
    pl.when(i == nb)(functools.partial(step, nb % 2, 1 - nb % 2, False, True))


def _scan_call(xc, xc_ctx, wg, bg, lam, tm):
    bsz, length, ch = xc.shape
    clen = xc_ctx.shape[1]
    nb = length // tm
    n_chains = N_DIRS * bsz
    assert clen <= tm and (tm // SUBLANES) % (n_chains * GATE_TILES) == 0
    rows8 = tm // SUBLANES
    slab = (n_chains, rows8, LANE_GROUPS * SUBLANES, LANES)
    gi = lambda i: jnp.minimum(i, nb - 1)
    si = lambda i: jnp.maximum(i - 1, 0)
    return pl.pallas_call(
        functools.partial(_scan_kernel, tm=tm, nb=nb, clen=clen, bsz=bsz),
        grid=(nb + 1,),
        in_specs=[pl.BlockSpec((bsz, tm, ch), lambda i: (0, gi(i), 0)),
                  pl.BlockSpec((bsz, tm, ch), lambda i: (0, nb - 1 - gi(i), 0)),
                  _const_spec((bsz, clen, ch)),
                  _const_spec((N_DIRS, GATE_TILES, GATE_TILE, 2 * GATE_TILE)),
                  _const_spec((N_DIRS, 2, ch)),
                  _const_spec((N_DIRS, 1, ch))],
        out_specs=[pl.BlockSpec((bsz, tm, ch), lambda i: (0, si(i), 0)),
                   pl.BlockSpec((bsz, tm, ch), lambda i: (0, nb - 1 - si(i), 0))],
        out_shape=[jax.ShapeDtypeStruct((bsz, length, ch), BF16)] * 2,
        scratch_shapes=[pltpu.VMEM((2,) + slab, F32), pltpu.VMEM((2,) + slab, F32), pltpu.VMEM(slab, F32),
                        pltpu.VMEM((n_chains, LANE_GROUPS, LANES), F32)],
        compiler_params=pltpu.CompilerParams(dimension_semantics=("arbitrary",),
                                             vmem_limit_bytes=VMEM_LIMIT),
        name="scan",
    )(xc, xc, xc_ctx, wg, bg, lam)


def _attention(i, sink_ref, q_ref, k_ref, kp_ref, kn_ref, v_ref, vp_ref, vn_ref, kc_ref, vc_ref,
               ya_scr, s_scr, kcat, vcat, vctx, *, nqb, length):
    tq = nqb * BLOCK
    clen = kc_ref.shape[1]
    kcat[pl.ds(0, BLOCK), :] = kp_ref[0]
    kcat[pl.ds(BLOCK, tq), :] = k_ref[0]
    kcat[pl.ds(BLOCK + tq, BLOCK), :] = kn_ref[0]
    for g in range(N_KV_HEADS):
        gs = slice(g * HEAD_DIM, (g + 1) * HEAD_DIM)
        vs = slice(2 * g * HEAD_DIM, (2 * g + 1) * HEAD_DIM)
        os_ = slice((2 * g + 1) * HEAD_DIM, (2 * g + 2) * HEAD_DIM)
        vcat[pl.ds(0, BLOCK), vs] = vp_ref[0, :, gs]
        vcat[pl.ds(BLOCK, tq), vs] = v_ref[0, :, gs]
        vcat[pl.ds(BLOCK + tq, BLOCK), vs] = vn_ref[0, :, gs]
        vcat[:, os_] = jnp.ones((tq + 2 * BLOCK, HEAD_DIM), BF16)
        vctx[:, vs] = vc_ref[0, :, gs]
        vctx[:, os_] = jnp.ones((clen, HEAD_DIM), BF16)

    qi = lax.broadcasted_iota(jnp.int32, (BLOCK, 3 * BLOCK), 0)
    kj = lax.broadcasted_iota(jnp.int32, (BLOCK, 3 * BLOCK), 1)
    rel = kj - qi
    band = (rel >= BLOCK - WINDOW) & (rel <= BLOCK + WINDOW)
    nt = (((1,), (1,)), ((), ()))
    rows = Q_PER_KV * BLOCK
    masks = []
    for jq in range(nqb):
        kpos = (i * nqb + jq - 1) * BLOCK + kj
        masks.append((band & (kpos >= 0) & (kpos < length))[None])

    n_loc = 3 * BLOCK

    def scores(jq, g, buf):
        gs = slice(g * HEAD_DIM, (g + 1) * HEAD_DIM)
        q4 = jnp.concatenate(
            [q_ref[0, pl.ds(jq * BLOCK, BLOCK), h * HEAD_DIM:(h + 1) * HEAD_DIM]
             for h in range(g * Q_PER_KV, (g + 1) * Q_PER_KV)], axis=0)
        s_scr[buf, :, 0:n_loc] = lax.dot_general(q4, kcat[pl.ds(jq * BLOCK, n_loc), gs], nt,
                                                 preferred_element_type=F32)
        s_scr[buf, :, n_loc:n_loc + clen] = lax.dot_general(q4, kc_ref[0, :, gs], nt, preferred_element_type=F32)

    def softmax_values(jq, g, buf):
        vos = slice(2 * g * HEAD_DIM, (2 * g + 2) * HEAD_DIM)
        heads = [g * Q_PER_KV + hh for hh in range(Q_PER_KV)]
        sink = jnp.concatenate([jnp.full((BLOCK, 1), sink_ref[h] * LOG2E, F32) for h in heads], axis=0)
        cols = []
        for c in range(n_loc // BLOCK):
            s = s_scr[buf, :, c * BLOCK:(c + 1) * BLOCK]
            if c != 1:
                s = jnp.where(masks[jq][:, :, c * BLOCK:(c + 1) * BLOCK], s.reshape(Q_PER_KV, BLOCK, BLOCK),
                              NEG_INF).reshape(rows, BLOCK)
            cols.append(s)
        cols += [s_scr[buf, :, n_loc + c * LANES:n_loc + (c + 1) * LANES] for c in range(clen // LANES)]
        m = jnp.maximum(jnp.max(functools.reduce(jnp.maximum, cols), axis=-1, keepdims=True), sink)
        p = [jnp.exp2(s - m).astype(BF16) for s in cols]
        p_loc = jnp.concatenate(p[:n_loc // BLOCK], axis=1)
        p_ctx = jnp.concatenate(p[n_loc // BLOCK:], axis=1)
        o2 = (jnp.dot(p_loc, vcat[pl.ds(jq * BLOCK, n_loc), vos], preferred_element_type=F32)
              + jnp.dot(p_ctx, vctx[:, vos], preferred_element_type=F32))
        denom = o2[:, HEAD_DIM:] + jnp.exp2(sink - m)
        o = o2[:, :HEAD_DIM] * (1.0 / denom)
        for hh, h in enumerate(heads):
            ya_scr[pl.ds(jq * BLOCK, BLOCK), h * HEAD_DIM:(h + 1) * HEAD_DIM] = (
                o[hh * BLOCK:(hh + 1) * BLOCK].astype(ya_scr.dtype))

    order = [(jq, g) for jq in range(nqb) for g in range(N_KV_HEADS)]
    return [(functools.partial(scores, jq, g, u % 2), functools.partial(softmax_values, jq, g, u % 2))
            for u, (jq, g) in enumerate(order)]


def _mix_kernel(*refs, nqb, length, n_steps):
    j = pl.program_id(0)
    run = functools.partial(_mix_streams, refs, j, j % 2, nqb=nqb, length=length, n_steps=n_steps)
    pl.when(j == 0)(functools.partial(run, True, False))
    pl.when((j > 0) & (j < n_steps - 1))(functools.partial(run, True, True))
    pl.when(j == n_steps - 1)(functools.partial(run, False, True))


def _mix_streams(refs, j, slot, attend, merge, *, nqb, length, n_steps):
    (sink_ref, q_ref, k_ref, kp_ref, kn_ref, v_ref, vp_ref, vn_ref, kc_ref, vc_ref,
     hf_ref, hb_ref, rg_ref, ga_ref, gb_ref, x_ref, mod_ref, wr_ref, wa_ref, wo_ref, nw_ref,
     x1_ref, h2_ref, ya_scr, yr_scr, s_scr, kcat, vcat, vctx) = refs
    nb = length // (nqb * BLOCK)
    units = []
    if attend:
        units = _attention(jnp.minimum(j, n_steps - 2) % nb, sink_ref, q_ref, k_ref, kp_ref, kn_ref, v_ref, vp_ref,
                           vn_ref, kc_ref, vc_ref, ya_scr.at[slot], s_scr, kcat, vcat, vctx, nqb=nqb,
                           length=length)

    state = {}

    def rnn_in():
        y_rnn = (hf_ref[0].astype(F32) + hb_ref[0].astype(F32)) * rg_ref[0].astype(F32)
        yr_scr[...] = y_rnn.astype(BF16)

    def branch_piece(p, w_ref, src, key):
        cs = slice(p * MXU_DIM, (p + 1) * MXU_DIM)
        state[key, p] = jnp.dot(src(), w_ref[:, cs], preferred_element_type=F32)

    def out_piece(p):
        cs = slice(p * MXU_DIM, (p + 1) * MXU_DIM)
        y = (ga_ref[0, :, cs].astype(F32) * state.pop(("r", p))
             + gb_ref[0, :, cs].astype(F32) * state.pop(("a", p)))
        part = jnp.dot(y.astype(BF16), wo_ref[cs, :], preferred_element_type=F32)
        state["z"] = part if "z" not in state else state["z"] + part

    n_p = D_MODEL // MXU_DIM
    att = [functools.partial(branch_piece, p, wa_ref, lambda: ya_scr[1 - slot], "a") for p in range(n_p)]
    rnn = [functools.partial(branch_piece, p, wr_ref, lambda: yr_scr[...], "r") for p in range(n_p)]
    outs = [functools.partial(out_piece, p) for p in range(n_p)]
    fillers = [rnn_in] + att + rnn[:n_p - 1] + [outs[0], rnn[n_p - 1]] + outs[1:] if merge else []
    per_unit = [2, 2, 1, 2, 1, 2, 1, 2]
    assert sum(per_unit) == 3 * n_p + 1 and len(per_unit) == nqb * N_KV_HEADS
    if units:
        units[0][0]()
    for u, (_, softmax_values) in enumerate(units):
        if u + 1 < len(units):
            units[u + 1][0]()
        for f in fillers[:per_unit[u]]:
            f()
        fillers = fillers[per_unit[u]:]
        softmax_values()
    for f in fillers:
        f()

    if merge:
        x1 = x_ref[0] + mod_ref[0, 2:3, :] * state["z"]
        x1_ref[0] = x1
        ms = jnp.mean(x1 * x1, axis=-1, keepdims=True)
        n2 = x1 * lax.rsqrt(ms + EPS) * nw_ref[...]
        h2_ref[0] = (n2 * (1.0 + mod_ref[0, 4:5, :]) + mod_ref[0, 3:4, :]).astype(h2_ref.dtype)


def _mix_call(sink, q, k, v, kc, vc, hf, hb, rg, ga, gb, x, mod, w_o_rnn, w_o_attn, w_out, nw2, nqb):
    bsz, length, _ = x.shape
    clen = kc.shape[1]
    tm = nqb * BLOCK
    n_blk = length // BLOCK
    nb = length // tm
    n_steps = bsz * nb + 1
    att = lambda j: divmod(jnp.minimum(j, n_steps - 2), nb)
    mrg = lambda j: divmod(jnp.maximum(j - 1, 0), nb)
    main = lambda j: (att(j)[0], att(j)[1], 0)
    prev = lambda j: (att(j)[0], jnp.maximum(att(j)[1] * nqb - 1, 0), 0)
    nxt = lambda j: (att(j)[0], jnp.minimum((att(j)[1] + 1) * nqb, n_blk - 1), 0)
    kv_specs = [pl.BlockSpec((1, tm, KV_WIDTH), main),
                pl.BlockSpec((1, BLOCK, KV_WIDTH), prev),
                pl.BlockSpec((1, BLOCK, KV_WIDTH), nxt)]
    ctx_spec = pl.BlockSpec((1, clen, KV_WIDTH), lambda j: (att(j)[0], 0, 0))
    blk = pl.BlockSpec((1, tm, D_MODEL), lambda j: (mrg(j)[0], mrg(j)[1], 0))
    return pl.pallas_call(
        functools.partial(_mix_kernel, nqb=nqb, length=length, n_steps=n_steps),
        grid=(n_steps,),
        in_specs=[pl.BlockSpec(memory_space=pltpu.SMEM),
                  pl.BlockSpec((1, tm, ATTN_WIDTH), main)] + kv_specs + kv_specs + [ctx_spec, ctx_spec]
                 + [blk, blk, blk, blk, blk, blk,
                    pl.BlockSpec((1, N_MOD, D_MODEL), lambda j: (mrg(j)[0], 0, 0)),
                    _const_spec((D_RNN, D_MODEL)), _const_spec((ATTN_WIDTH, D_MODEL)),
                    _const_spec((D_MODEL, D_MODEL)), _const_spec((1, D_MODEL))],
        out_specs=[blk, blk],
        out_shape=[jax.ShapeDtypeStruct((bsz, length, D_MODEL), F32),
                   jax.ShapeDtypeStruct((bsz, length, D_MODEL), BF16)],
        scratch_shapes=[pltpu.VMEM((2, tm, ATTN_WIDTH), BF16),
                        pltpu.VMEM((tm, D_RNN), BF16),
                        pltpu.VMEM((2, Q_PER_KV * BLOCK, 3 * BLOCK + clen), F32),
                        pltpu.VMEM((tm + 2 * BLOCK, KV_WIDTH), BF16),
                        pltpu.VMEM((tm + 2 * BLOCK, 2 * KV_WIDTH), BF16),
                        pltpu.VMEM((clen, 2 * KV_WIDTH), BF16)],
        compiler_params=pltpu.CompilerParams(dimension_semantics=("arbitrary",),
                                             vmem_limit_bytes=VMEM_LIMIT),
        name="mix",
    )(sink, q, k, k, k, v, v, v, kc, vc, hf, hb, rg, ga, gb, x, mod, w_o_rnn, w_o_attn, w_out, nw2)


def _ffn_kernel(h_ref, hp_ref, hn_ref, x1_ref, mod_ref, wu_ref, cw_ref, cb_ref, wd_ref, fw_ref,
                o_ref, hbuf, ua_scr, uv_scr, g_scr, *, tm, nb, n_steps):
    j = pl.program_id(0)
    i = jnp.minimum(j, n_steps - 2) % nb
    ring = j % 2

    def stage_rows():
        hbuf[pl.ds(0, HALO), :] = jnp.where(i > 0, hp_ref[0], jnp.zeros_like(hp_ref[0]))
        hbuf[pl.ds(HALO, tm), :] = h_ref[0]
        hbuf[pl.ds(HALO + tm, HALO), :] = jnp.where(i < nb - 1, hn_ref[0], jnp.zeros_like(hn_ref[0]))

    rows = tm + 2 * HALO

    n_chunks = D_FF // FFN_CHUNK
    pieces = FFN_CHUNK // MXU_DIM
    lg_per_piece = MXU_DIM // LANES

    def up_piece(u_scr, c, p, c0):
        u = jnp.dot(hbuf[...], wu_ref[:, c0 + p * MXU_DIM:c0 + (p + 1) * MXU_DIM], preferred_element_type=F32)
        for q in range(lg_per_piece):
            u_scr[c % 2, p * lg_per_piece + q, pl.ds(0, rows, stride=2), :] = u[:, q * LANES:(q + 1) * LANES]

    def conv(u_scr, slot, lg, c0):
        cs = slice(c0 + lg * LANES, c0 + (lg + 1) * LANES)
        taps = [u_scr[slot, lg, pl.ds(2 * (HALO - FFN_CONV_W // 2 + k), tm, stride=2), :] for k in range(FFN_CONV_W)]
        return cb_ref[:, cs] + sum(cw_ref[k:k + 1, cs] * taps[k] for k in range(FFN_CONV_W))

    def act_piece(c, lg):
        ca, cv = c * FFN_CHUNK, D_FF + c * FFN_CHUNK
        g = _gelu(conv(ua_scr, c % 2, lg, ca)) * conv(uv_scr, c % 2, lg, cv)
        g_scr[ring, :, ca + lg * LANES:ca + (lg + 1) * LANES] = g.astype(BF16)

    half = D_MODEL // 2

    def run(activate, project):
        z = [None, None]

        def down_piece(c, hh):
            ks = slice(c * FFN_CHUNK, (c + 1) * FFN_CHUNK)
            part = jnp.dot(g_scr[1 - ring, :, ks], wd_ref[ks, hh * half:(hh + 1) * half],
                           preferred_element_type=F32)
            z[hh] = part if z[hh] is None else z[hh] + part

        downs = [functools.partial(down_piece, c, hh) for c in range(n_chunks) for hh in range(2)] if project else []
        n_down = [1] * (n_chunks - 1) + [len(downs)]
        if activate:
            stage_rows()
            for p in range(pieces):
                up_piece(ua_scr, 0, p, 0)
                up_piece(uv_scr, 0, p, D_FF)
        for c in range(n_chunks):
            mm, acts = [], []
            if activate and c + 1 < n_chunks:
                for p in range(pieces):
                    mm.append(functools.partial(up_piece, ua_scr, c + 1, p, (c + 1) * FFN_CHUNK))
                    mm.append(functools.partial(up_piece, uv_scr, c + 1, p, D_FF + (c + 1) * FFN_CHUNK))
            mm += downs[:n_down[c]]
            downs = downs[n_down[c]:]
            if activate:
                acts = [functools.partial(act_piece, c, lg) for lg in range(FFN_CHUNK // LANES)]
            while mm or acts:
                if mm:
                    mm.pop(0)()
                if acts:
                    acts.pop(0)()
        if project:
            x2 = x1_ref[0] + mod_ref[0, 5:6, :] * jnp.concatenate(z, axis=1)
            ms = jnp.mean(x2 * x2, axis=-1, keepdims=True)
            o_ref[0] = x2 * lax.rsqrt(ms + EPS) * fw_ref[...]

    pl.when(j == 0)(functools.partial(run, True, False))
    pl.when((j > 0) & (j < n_steps - 1))(functools.partial(run, True, True))
    pl.when(j == n_steps - 1)(functools.partial(run, False, True))


def _ffn_call(h2, x1, mod, w_up, conv_w, conv_b, w_down, fw, tm):
    bsz, length, _ = x1.shape
    nb = length // tm
    n_steps = bsz * nb + 1
    hb = tm // HALO
    n_hb = length // HALO
    act = lambda j: divmod(jnp.minimum(j, n_steps - 2), nb)
    dwn = lambda j: divmod(jnp.maximum(j - 1, 0), nb)
    tok_out = lambda j: (dwn(j)[0], dwn(j)[1], 0)
    return pl.pallas_call(
        functools.partial(_ffn_kernel, tm=tm, nb=nb, n_steps=n_steps),
        grid=(n_steps,),
        in_specs=[pl.BlockSpec((1, tm, D_MODEL), lambda j: (act(j)[0], act(j)[1], 0)),
                  pl.BlockSpec((1, HALO, D_MODEL), lambda j: (act(j)[0], jnp.maximum(act(j)[1] * hb - 1, 0), 0)),
                  pl.BlockSpec((1, HALO, D_MODEL),
                               lambda j: (act(j)[0], jnp.minimum((act(j)[1] + 1) * hb, n_hb - 1), 0)),
                  pl.BlockSpec((1, tm, D_MODEL), tok_out),
                  pl.BlockSpec((1, N_MOD, D_MODEL), lambda j: (dwn(j)[0], 0, 0)),
                  _const_spec((D_MODEL, 2 * D_FF)), _const_spec((FFN_CONV_W, 2 * D_FF)),
                  _const_spec((1, 2 * D_FF)), _const_spec((D_FF, D_MODEL)), _const_spec((1, D_MODEL))],
        out_specs=pl.BlockSpec((1, tm, D_MODEL), tok_out),
        out_shape=jax.ShapeDtypeStruct((bsz, length, D_MODEL), F32),
        scratch_shapes=[pltpu.VMEM((tm + 2 * HALO, D_MODEL), BF16),
                        pltpu.VMEM((2, FFN_CHUNK // LANES, 2 * (tm + 2 * HALO), LANES), F32),
                        pltpu.VMEM((2, FFN_CHUNK // LANES, 2 * (tm + 2 * HALO), LANES), F32),
                        pltpu.VMEM((2, tm, D_FF), BF16)],
        compiler_params=pltpu.CompilerParams(dimension_semantics=("arbitrary",),
                                             vmem_limit_bytes=VMEM_LIMIT),
        name="ffn",
    )(h2, h2, h2, x1, mod, w_up, conv_w, conv_b, w_down, fw)


def _rope_tables(length):
    pos = np.arange(length)
    row, col = pos // GRID_W, pos % GRID_W
    half = HEAD_DIM // 2
    inv = ROPE_THETA ** (-np.arange(0, half, 2, dtype=np.float64) / half)
    ar, ac = row[:, None] * inv, col[:, None] * inv
    cos = np.concatenate([np.cos(ar), np.cos(ar), np.cos(ac), np.cos(ac)], axis=1)
    sin = np.concatenate([-np.sin(ar), np.sin(ar), -np.sin(ac), np.sin(ac)], axis=1)
    return jnp.asarray(cos, F32), jnp.asarray(sin, F32)


def _gate_weights(w_a, w_x):
    per = GATE_TILE // LRU_BLOCK
    eye = jnp.eye(per, dtype=F32)

    def bd(w):
        w = w.reshape(N_DIRS, GATE_TILES, per, LRU_BLOCK, LRU_BLOCK)
        return jnp.einsum('dtpcn,pq->dtpcqn', w, eye).reshape(N_DIRS, GATE_TILES, GATE_TILE, GATE_TILE)

    return jnp.concatenate([bd(w_a), bd(w_x)], axis=-1).astype(BF16)


def kernel(x, c, ctx, c_ctx, w_mod, b_mod, norm1_w, w_in, lru_conv_w, lru_conv_b, gate_a_w, gate_a_b,
           gate_x_w, gate_x_b, lru_lambda, sink_logit, w_o_rnn, w_o_attn, w_out, norm2_w, w_up,
           ffn_conv_w, ffn_conv_b, w_down, final_norm_w):
    bsz, length, _ = x.shape
    l = 0

    cs = jnp.concatenate([c, c_ctx[None], jnp.zeros((SUBLANES - bsz - 1, D_MODEL), F32)], axis=0)
    mod = _mod_call(cs, w_mod[l], b_mod[l][None]).reshape(SUBLANES, N_MOD, D_MODEL)

    w_in_b = w_in[l].astype(BF16)
    nw1 = norm1_w[l][None]
    cw, cb = lru_conv_w[l], lru_conv_b[l][None]

    plan_x = [(COL_R, D_RNN, "conv"), (COL_RG, D_RNN, "gelu"), (COL_Q, ATTN_WIDTH, "rope_q"),
              (COL_K, KV_WIDTH, "rope_k"), (COL_V, KV_WIDTH, "bf16"), (COL_GA, D_MODEL, "sigmoid"),
              (COL_GB, D_MODEL, "sigmoid")]
    xc, rg, q, k, v, ga, gb = _inproj_call(x, mod, 0, nw1, w_in_b, cw, cb, _rope_tables(length), plan_x, tm=512)
    plan_c = [(COL_R, D_RNN, "conv"), (COL_K, KV_WIDTH, "bf16"), (COL_V, KV_WIDTH, "bf16")]
    xc_ctx, kc, vc = _inproj_call(ctx, mod, bsz, nw1, w_in_b, cw, cb, None, plan_c, tm=ctx.shape[1])

    wg = _gate_weights(gate_a_w[l], gate_x_w[l])
    bg = jnp.stack([gate_a_b[l], gate_x_b[l]], axis=1)
    hf, hb = _scan_call(xc, xc_ctx, wg, bg, lru_lambda[l][:, None, :], tm=256)

    x1, h2 = _mix_call(sink_logit[l], q, k, v, kc, vc, hf, hb, rg, ga, gb, x, mod, w_o_rnn[l].astype(BF16),
                       w_o_attn[l].astype(BF16), w_out[l].astype(BF16), norm2_w[l][None], nqb=4)
    return _ffn_call(h2, x1, mod, w_up[l].astype(BF16), ffn_conv_w[l], ffn_conv_b[l][None],
                     w_down[l].astype(BF16), final_norm_w[None], tm=512)
```

```python
import functools

import numpy as np
import jax
import jax.numpy as jnp
from jax import lax
from jax.experimental import pallas as pl
from jax.experimental.pallas import tpu as pltpu

F32 = jnp.float32
BF16 = jnp.bfloat16

D_MODEL = 1024
N_HEADS = 8
N_KV_HEADS = 2
HEAD_DIM = 128
Q_PER_KV = N_HEADS // N_KV_HEADS
ATTN_WIDTH = N_HEADS * HEAD_DIM
KV_WIDTH = N_KV_HEADS * HEAD_DIM
WINDOW = 128
BLOCK = 128
GRID_W = 64
ROPE_THETA = 10000.0
D_RNN = D_MODEL
N_LRU_BLOCKS = 16
LRU_BLOCK = D_RNN // N_LRU_BLOCKS
LRU_C = 8.0
LRU_CONV_W = 4
LRU_CONV_LEFT = 2
N_DIRS = 2
D_FF = 3 * D_MODEL
FFN_CONV_W = 3
N_MOD = 6
EPS = 1e-6
NEG_INF = -1e30

COL_R = 0
COL_RG = COL_R + D_RNN
COL_Q = COL_RG + D_RNN
COL_K = COL_Q + ATTN_WIDTH
COL_V = COL_K + KV_WIDTH
COL_GA = COL_V + KV_WIDTH
COL_GB = COL_GA + D_MODEL
D_IN = COL_GB + D_MODEL

SUBLANES = 8
BF16_SUBLANES = 16
LANES = 128
MXU_DIM = 256
VMEM_LIMIT = 56 * 1024 * 1024

LANE_GROUPS = D_RNN // LANES
GATE_TILE = MXU_DIM
GATE_TILES = D_RNN // GATE_TILE
FFN_CHUNK = 512
HALO = BF16_SUBLANES
LOG2E = float(np.log2(np.e))
assert BLOCK == LANES and WINDOW >= BLOCK - 1
TINY = 1e-30


GELU_K1 = float(np.sqrt(2.0 / np.pi))
GELU_K2 = GELU_K1 * 0.044715


def _gelu(v):
    return (0.5 * v) * (1.0 + jnp.tanh(v * (GELU_K1 + GELU_K2 * (v * v))))


def _sigmoid(v):
    return 0.5 * jnp.tanh(0.5 * v) + 0.5


def _const_spec(shape):
    nd = len(shape)
    return pl.BlockSpec(shape, lambda *_: (0,) * nd, pipeline_mode=pl.Buffered(1))


def _halo_specs(tm, length, width):
    hb = tm // HALO
    n_hb = length // HALO
    prev = pl.BlockSpec((1, HALO, width), lambda b, i: (b, jnp.maximum(i * hb - 1, 0), 0))
    nxt = pl.BlockSpec((1, HALO, width), lambda b, i: (b, jnp.minimum((i + 1) * hb, n_hb - 1), 0))
    return prev, nxt


def _mod_kernel(c_ref, w_ref, b_ref, o_ref):
    c = c_ref[...]
    s = c * _sigmoid(c)
    o_ref[...] = jnp.dot(s, w_ref[...], preferred_element_type=F32) + b_ref[...]


def _mod_call(cs, w_mod, b_mod):
    n_out = w_mod.shape[1]
    tn = n_out // 4
    return pl.pallas_call(
        _mod_kernel,
        grid=(n_out // tn,),
        in_specs=[pl.BlockSpec((SUBLANES, D_MODEL), lambda j: (0, 0)),
                  pl.BlockSpec((D_MODEL, tn), lambda j: (0, j)),
                  pl.BlockSpec((1, tn), lambda j: (0, j))],
        out_specs=pl.BlockSpec((SUBLANES, tn), lambda j: (0, j)),
        out_shape=jax.ShapeDtypeStruct((SUBLANES, n_out), F32),
        compiler_params=pltpu.CompilerParams(dimension_semantics=("arbitrary",),
                                             vmem_limit_bytes=VMEM_LIMIT),
        name="mod",
    )(cs, w_mod, b_mod)


def _inproj_kernel(*refs, plan, rope, tm, nb, n_steps):
    x_ref, xp_ref, xn_ref, mod_ref, nw_ref, w_ref, cw_ref, cb_ref = refs[:8]
    n_in = 10 if rope else 8
    out_refs = refs[n_in:n_in + len(plan)]
    h_scr, r_scr = refs[-2:]
    j = pl.program_id(0)
    i = jnp.minimum(j, n_steps - 2) % nb
    slot = j % 2
    h_in = h_scr.at[1 - slot]

    def norm_mod(x):
        ms = jnp.mean(x * x, axis=-1, keepdims=True)
        y = x * lax.rsqrt(ms + EPS) * nw_ref[...]
        return y * (1.0 + mod_ref[0, 1:2, :]) + mod_ref[0, 0:1, :]

    def norm_stage():
        h_scr[slot, pl.ds(0, HALO), :] = jnp.where(i > 0, norm_mod(xp_ref[0]), 0.0).astype(BF16)
        h_scr[slot, pl.ds(HALO, tm), :] = norm_mod(x_ref[0]).astype(BF16)
        h_scr[slot, pl.ds(HALO + tm, HALO), :] = jnp.where(i < nb - 1, norm_mod(xn_ref[0]), 0.0).astype(BF16)

    def piece(c0, kind, o_ref, p):
        cs = slice(p * MXU_DIM, (p + 1) * MXU_DIM)
        wc = slice(c0 + p * MXU_DIM, c0 + (p + 1) * MXU_DIM)
        if kind == "conv":
            r = jnp.dot(h_in[...], w_ref[:, wc], preferred_element_type=F32)
            for q in range(MXU_DIM // LANES):
                lg = p * (MXU_DIM // LANES) + q
                ls = slice(lg * LANES, (lg + 1) * LANES)
                r_scr[lg, pl.ds(0, tm + 2 * HALO, stride=2), :] = r[:, q * LANES:(q + 1) * LANES]
                taps = [r_scr[lg, pl.ds(2 * (HALO - LRU_CONV_LEFT + k), tm, stride=2), :] for k in range(LRU_CONV_W)]
                o_ref[0, :, ls] = 0.5 * cb_ref[:, ls] + sum((0.5 * cw_ref[k:k + 1, ls]) * taps[k]
                                                            for k in range(LRU_CONV_W))
            return
        res = jnp.dot(h_in[pl.ds(HALO, tm), :], w_ref[:, wc], preferred_element_type=F32)
        if kind in ("rope_q", "rope_k"):
            scale = HEAD_DIM ** -0.5 * LOG2E if kind == "rope_q" else None
            cos, sin = refs[8][...], refs[9][...]
            lane = lax.broadcasted_iota(jnp.int32, (tm, HEAD_DIM), 1)
            low = (lane % (HEAD_DIM // 2)) < (HEAD_DIM // 4)
            for hh in range(MXU_DIM // HEAD_DIM):
                v = res[:, hh * HEAD_DIM:(hh + 1) * HEAD_DIM]
                partner = jnp.where(low, pltpu.roll(v, HEAD_DIM - HEAD_DIM // 4, 1), pltpu.roll(v, HEAD_DIM // 4, 1))
                rot = v * cos + partner * sin
                if scale is not None:
                    rot = rot * scale
                o_ref[0, :, p * MXU_DIM + hh * HEAD_DIM:p * MXU_DIM + (hh + 1) * HEAD_DIM] = rot.astype(o_ref.dtype)
        else:
            act = {"gelu": _gelu, "sigmoid": _sigmoid, "bf16": lambda t: t}[kind]
            o_ref[0, :, cs] = act(res).astype(o_ref.dtype)

    def project():
        by_kind = {}
        for (c0, width, kind), o_ref in zip(plan, out_refs):
            by_kind.setdefault(kind, []).extend(
                functools.partial(piece, c0, kind, o_ref, p) for p in range(width // MXU_DIM))
        take = lambda kind: by_kind[kind].pop(0)() if by_kind.get(kind) else None
        take("conv"), take("sigmoid")
        norm_stage()
        take("sigmoid"), take("bf16")
        while by_kind.get("conv"):
            take("conv"), take("sigmoid"), take("sigmoid"), take("bf16")
        while by_kind.get("rope_q") or by_kind.get("rope_k"):
            take("rope_q"), take("gelu"), take("rope_k"), take("bf16")
        for kind in list(by_kind):
            while by_kind[kind]:
                take(kind)

    pl.when(j == 0)(norm_stage)
    pl.when(j > 0)(project)


def _inproj_call(x, mod, mod_row0, nw, w_in, conv_w, conv_b, tables, plan, tm):
    bsz, length, _ = x.shape
    rope = tables is not None
    nb = length // tm
    n_steps = bsz * nb + 1
    hb = tm // HALO
    n_hb = length // HALO
    nrm = lambda j: divmod(jnp.minimum(j, n_steps - 2), nb)
    prj = lambda j: divmod(jnp.maximum(j - 1, 0), nb)
    tok_out = lambda j: (prj(j)[0], prj(j)[1], 0)
    in_specs = [pl.BlockSpec((1, tm, D_MODEL), lambda j: (nrm(j)[0], nrm(j)[1], 0)),
                pl.BlockSpec((1, HALO, D_MODEL), lambda j: (nrm(j)[0], jnp.maximum(nrm(j)[1] * hb - 1, 0), 0)),
                pl.BlockSpec((1, HALO, D_MODEL),
                             lambda j: (nrm(j)[0], jnp.minimum((nrm(j)[1] + 1) * hb, n_hb - 1), 0)),
                pl.BlockSpec((1, N_MOD, D_MODEL), lambda j: (nrm(j)[0] * (mod_row0 == 0) + mod_row0, 0, 0)),
                _const_spec((1, D_MODEL)),
                _const_spec((D_MODEL, D_IN)),
                _const_spec((LRU_CONV_W, D_RNN)),
                _const_spec((1, D_RNN))]
    args = [x, x, x, mod, nw, w_in, conv_w, conv_b]
    if rope:
        in_specs += [pl.BlockSpec((tm, HEAD_DIM), lambda j: (prj(j)[1], 0))] * 2
        args += list(tables)
    out_specs, out_shape = [], []
    for (_, width, kind) in plan:
        dt = F32 if kind == "conv" else BF16
        out_specs.append(pl.BlockSpec((1, tm, width), tok_out))
        out_shape.append(jax.ShapeDtypeStruct((bsz, length, width), dt))
    return pl.pallas_call(
        functools.partial(_inproj_kernel, plan=tuple(plan), rope=rope, tm=tm, nb=nb, n_steps=n_steps),
        grid=(n_steps,),
        in_specs=in_specs, out_specs=out_specs, out_shape=out_shape,
        scratch_shapes=[pltpu.VMEM((2, tm + 2 * HALO, D_MODEL), BF16),
                        pltpu.VMEM((D_RNN // LANES, 2 * (tm + 2 * HALO), LANES), F32)],
        compiler_params=pltpu.CompilerParams(dimension_semantics=("arbitrary",),
                                             vmem_limit_bytes=VMEM_LIMIT),
        name="inproj_x" if rope else "inproj_ctx",
    )(*args)


def _scan_kernel(xf_ref, xb_ref, xc_ref, wg_ref, bg_ref, lam_ref, hf_ref, hb_ref,
                 a3, b3, h3, hc_scr, *, tm, nb, clen, bsz):
    i = pl.program_id(0)
    chains = [(d, b) for d in range(N_DIRS) for b in range(bsz)]
    lg_per_tile = GATE_TILE // LANES

    def gate_piece(src_ref, ws, d, b, ch, kt, n):
        lam = lam_ref[d]
        softplus = jnp.maximum(-lam, 0.0) + jnp.log(1.0 + jnp.exp(-jnp.abs(lam)))
        hd2 = (-0.5 * LRU_C * LOG2E) * softplus
        cs = slice(kt * GATE_TILE, (kt + 1) * GATE_TILE)
        xk = src_ref[b, pl.ds(0, n), cs]
        g = jnp.dot(xk.astype(BF16), wg_ref[d, kt], preferred_element_type=F32)
        t_r = jnp.tanh(g[:, :GATE_TILE] + 0.5 * bg_ref[d, 0:1, cs])
        t_i = jnp.tanh(g[:, GATE_TILE:] + 0.5 * bg_ref[d, 1:2, cs])
        a = jnp.exp2(hd2[:, cs] * t_r + hd2[:, cs])
        y = 1.0 - a * a
        bb = (y * lax.rsqrt(jnp.maximum(y, TINY)) * xk) * (t_i + 1.0)
        for gg in range(lg_per_tile):
            lg = kt * lg_per_tile + gg
            ls = slice(gg * LANES, (gg + 1) * LANES)
            a3[ws, ch, pl.ds(0, n // SUBLANES), pl.ds(lg, SUBLANES, stride=LANE_GROUPS), :] = (
                a[:, ls].reshape(n // SUBLANES, SUBLANES, LANES))
            b3[ws, ch, pl.ds(0, n // SUBLANES), pl.ds(lg, SUBLANES, stride=LANE_GROUPS), :] = (
                bb[:, ls].reshape(n // SUBLANES, SUBLANES, LANES))

    def scan_group(rs, n8, tau, hs, keep):
        for rho in range(SUBLANES):
            for ch, (d, b) in enumerate(chains):
                tt, rr = (tau, rho) if d == 0 else (n8 - 1 - tau, SUBLANES - 1 - rho)
                win = pl.ds(rr * LANE_GROUPS, LANE_GROUPS)
                h = a3[rs, ch, tt, win, :] * hs[ch] + b3[rs, ch, tt, win, :]
                hs[ch] = h
                if keep:
                    h3[ch, tt, win, :] = h

    def emit(rows):
        r0, r1 = rows
        for ch, (d, b) in enumerate(chains):
            lo, hi = (r0, r1) if d == 0 else (tm - r1, tm - r0)
            o_ref = hf_ref if d == 0 else hb_ref
            for lg in range(LANE_GROUPS):
                slab = h3[ch, pl.ds(lo // SUBLANES, (hi - lo) // SUBLANES), pl.ds(lg, SUBLANES, stride=LANE_GROUPS), :]
                o_ref[b, pl.ds(lo, hi - lo), lg * LANES:(lg + 1) * LANES] = (
                    slab.reshape(hi - lo, LANES).astype(o_ref.dtype))

    @pl.when(i == 0)
    def _():
        for ch, (d, b) in enumerate(chains):
            for kt in range(GATE_TILES):
                gate_piece(xc_ref, 1, d, b, ch, kt, clen)

        def body(tau, hs):
            hs = list(hs)
            scan_group(1, clen // SUBLANES, tau, hs, False)
            return tuple(hs)

        hs = lax.fori_loop(0, clen // SUBLANES, body, (jnp.zeros((LANE_GROUPS, LANES), F32),) * len(chains))
        for ch in range(len(chains)):
            hc_scr[ch] = hs[ch]

    def step(ws, rs, gate, scan):
        n8 = tm // SUBLANES
        pieces = [(d, b, ch, kt) for ch, (d, b) in enumerate(chains) for kt in range(GATE_TILES)]
        per_piece = n8 // len(pieces)
        hs = [hc_scr[ch] for ch in range(len(chains))] if scan else None
        tau = 0
        for (d, b, ch, kt) in pieces:
            if gate:
                gate_piece(xf_ref if d == 0 else xb_ref, ws, d, b, ch, kt, tm)
            for _ in range(per_piece if scan else 0):
                scan_group(rs, n8, tau, hs, True)
                tau += 1
                if tau == n8 // 2:
                    emit((0, tm // 2))
        if scan:
            emit((tm // 2, tm))
            for ch in range(len(chains)):
                hc_scr[ch] = hs[ch]

    pl.when(i == 0)(functools.partial(step, 0, 1, True, False))
    for parity in range(2):
        pl.when((i > 0) & (i < nb) & (i % 2 == parity))(functools.partial(step, parity, 1 - parity, True, True))
    pl.when(i == nb)(functools.partial(step, nb % 2, 1 - nb % 2, False, True))


def _scan_call(xc, xc_ctx, wg, bg, lam, tm):
    bsz, length, ch = xc.shape
    clen = xc_ctx.shape[1]
    nb = length // tm
    n_chains = N_DIRS * bsz
    assert clen <= tm and (tm // SUBLANES) % (n_chains * GATE_TILES) == 0
    rows8 = tm // SUBLANES
    slab = (n_chains, rows8, LANE_GROUPS * SUBLANES, LANES)
    gi = lambda i: jnp.minimum(i, nb - 1)
    si = lambda i: jnp.maximum(i - 1, 0)
    return pl.pallas_call(
        functools.partial(_scan_kernel, tm=tm, nb=nb, clen=clen, bsz=bsz),
        grid=(nb + 1,),
        in_specs=[pl.BlockSpec((bsz, tm, ch), lambda i: (0, gi(i), 0)),
                  pl.BlockSpec((bsz, tm, ch), lambda i: (0, nb - 1 - gi(i), 0)),
                  _const_spec((bsz, clen, ch)),
                  _const_spec((N_DIRS, GATE_TILES, GATE_TILE, 2 * GATE_TILE)),
                  _const_spec((N_DIRS, 2, ch)),
                  _const_spec((N_DIRS, 1, ch))],
        out_specs=[pl.BlockSpec((bsz, tm, ch), lambda i: (0, si(i), 0)),
                   pl.BlockSpec((bsz, tm, ch), lambda i: (0, nb - 1 - si(i), 0))],
        out_shape=[jax.ShapeDtypeStruct((bsz, length, ch), BF16)] * 2,
        scratch_shapes=[pltpu.VMEM((2,) + slab, F32), pltpu.VMEM((2,) + slab, F32), pltpu.VMEM(slab, F32),
                        pltpu.VMEM((n_chains, LANE_GROUPS, LANES), F32)],
        compiler_params=pltpu.CompilerParams(dimension_semantics=("arbitrary",),
                                             vmem_limit_bytes=VMEM_LIMIT),
        name="scan",
    )(xc, xc, xc_ctx, wg, bg, lam)


def _attention(i, sink_ref, q_ref, k_ref, kp_ref, kn_ref, v_ref, vp_ref, vn_ref, kc_ref, vc_ref,
               ya_scr, s_scr, kcat, vcat, vctx, *, nqb, length):
    tq = nqb * BLOCK
    clen = kc_ref.shape[1]
    kcat[pl.ds(0, BLOCK), :] = kp_ref[0]
    kcat[pl.ds(BLOCK, tq), :] = k_ref[0]
    kcat[pl.ds(BLOCK + tq, BLOCK), :] = kn_ref[0]
    for g in range(N_KV_HEADS):
        gs = slice(g * HEAD_DIM, (g + 1) * HEAD_DIM)
        vs = slice(2 * g * HEAD_DIM, (2 * g + 1) * HEAD_DIM)
        os_ = slice((2 * g + 1) * HEAD_DIM, (2 * g + 2) * HEAD_DIM)
        vcat[pl.ds(0, BLOCK), vs] = vp_ref[0, :, gs]
        vcat[pl.ds(BLOCK, tq), vs] = v_ref[0, :, gs]
        vcat[pl.ds(BLOCK + tq, BLOCK), vs] = vn_ref[0, :, gs]
        vcat[:, os_] = jnp.ones((tq + 2 * BLOCK, HEAD_DIM), BF16)
        vctx[:, vs] = vc_ref[0, :, gs]
        vctx[:, os_] = jnp.ones((clen, HEAD_DIM), BF16)

    qi = lax.broadcasted_iota(jnp.int32, (BLOCK, 3 * BLOCK), 0)
    kj = lax.broadcasted_iota(jnp.int32, (BLOCK, 3 * BLOCK), 1)
    rel = kj - qi
    band = (rel >= BLOCK - WINDOW) & (rel <= BLOCK + WINDOW)
    nt = (((1,), (1,)), ((), ()))
    rows = Q_PER_KV * BLOCK
    masks = []
    for jq in range(nqb):
        kpos = (i * nqb + jq - 1) * BLOCK + kj
        masks.append((band & (kpos >= 0) & (kpos < length))[None])

    n_loc = 3 * BLOCK

    def scores(jq, g, buf):
        gs = slice(g * HEAD_DIM, (g + 1) * HEAD_DIM)
        q4 = jnp.concatenate(
            [q_ref[0, pl.ds(jq * BLOCK, BLOCK), h * HEAD_DIM:(h + 1) * HEAD_DIM]
             for h in range(g * Q_PER_KV, (g + 1) * Q_PER_KV)], axis=0)
        s_scr[buf, :, 0:n_loc] = lax.dot_general(q4, kcat[pl.ds(jq * BLOCK, n_loc), gs], nt,
                                                 preferred_element_type=F32)
        s_scr[buf, :, n_loc:n_loc + clen] = lax.dot_general(q4, kc_ref[0, :, gs], nt, preferred_element_type=F32)

    def softmax_values(jq, g, buf):
        vos = slice(2 * g * HEAD_DIM, (2 * g + 2) * HEAD_DIM)
        heads = [g * Q_PER_KV + hh for hh in range(Q_PER_KV)]
        sink = jnp.concatenate([jnp.full((BLOCK, 1), sink_ref[h] * LOG2E, F32) for h in heads], axis=0)
        cols = []
        for c in range(n_loc // BLOCK):
            s = s_scr[buf, :, c * BLOCK:(c + 1) * BLOCK]
            if c != 1:
                s = jnp.where(masks[jq][:, :, c * BLOCK:(c + 1) * BLOCK], s.reshape(Q_PER_KV, BLOCK, BLOCK),
                              NEG_INF).reshape(rows, BLOCK)
            cols.append(s)
        cols += [s_scr[buf, :, n_loc + c * LANES:n_loc + (c + 1) * LANES] for c in range(clen // LANES)]
        m = jnp.maximum(jnp.max(functools.reduce(jnp.maximum, cols), axis=-1, keepdims=True), sink)
        p = [jnp.exp2(s - m).astype(BF16) for s in cols]
        p_loc = jnp.concatenate(p[:n_loc // BLOCK], axis=1)
        p_ctx = jnp.concatenate(p[n_loc // BLOCK:], axis=1)
        o2 = (jnp.dot(p_loc, vcat[pl.ds(jq * BLOCK, n_loc), vos], preferred_element_type=F32)
              + jnp.dot(p_ctx, vctx[:, vos], preferred_element_type=F32))
        denom = o2[:, HEAD_DIM:] + jnp.exp2(sink - m)
        o = o2[:, :HEAD_DIM] * (1.0 / denom)
        for hh, h in enumerate(heads):
            ya_scr[pl.ds(jq * BLOCK, BLOCK), h * HEAD_DIM:(h + 1) * HEAD_DIM] = (
                o[hh * BLOCK:(hh + 1) * BLOCK].astype(ya_scr.dtype))

    order = [(jq, g) for jq in range(nqb) for g in range(N_KV_HEADS)]
    return [(functools.partial(scores, jq, g, u % 2), functools.partial(softmax_values, jq, g, u % 2))
            for u, (jq, g) in enumerate(order)]


def _mix_kernel(*refs, nqb, length, n_steps):
    j = pl.program_id(0)
    run = functools.partial(_mix_streams, refs, j, j % 2, nqb=nqb, length=length, n_steps=n_steps)
    pl.when(j == 0)(functools.partial(run, True, False))
    pl.when((j > 0) & (j < n_steps - 1))(functools.partial(run, True, True))
    pl.when(j == n_steps - 1)(functools.partial(run, False, True))


def _mix_streams(refs, j, slot, attend, merge, *, nqb, length, n_steps):
    (sink_ref, q_ref, k_ref, kp_ref, kn_ref, v_ref, vp_ref, vn_ref, kc_ref, vc_ref,
     hf_ref, hb_ref, rg_ref, ga_ref, gb_ref, x_ref, mod_ref, wr_ref, wa_ref, wo_ref, nw_ref,
     x1_ref, h2_ref, ya_scr, yr_scr, s_scr, kcat, vcat, vctx) = refs
    nb = length // (nqb * BLOCK)
    units = []
    if attend:
        units = _attention(jnp.minimum(j, n_steps - 2) % nb, sink_ref, q_ref, k_ref, kp_ref, kn_ref, v_ref, vp_ref,
                           vn_ref, kc_ref, vc_ref, ya_scr.at[slot], s_scr, kcat, vcat, vctx, nqb=nqb,
                           length=length)

    state = {}

    def rnn_in():
        y_rnn = (hf_ref[0].astype(F32) + hb_ref[0].astype(F32)) * rg_ref[0].astype(F32)
        yr_scr[...] = y_rnn.astype(BF16)

    def branch_piece(p, w_ref, src, key):
        cs = slice(p * MXU_DIM, (p + 1) * MXU_DIM)
        state[key, p] = jnp.dot(src(), w_ref[:, cs], preferred_element_type=F32)

    def out_piece(p):
        cs = slice(p * MXU_DIM, (p + 1) * MXU_DIM)
        y = (ga_ref[0, :, cs].astype(F32) * state.pop(("r", p))
             + gb_ref[0, :, cs].astype(F32) * state.pop(("a", p)))
        part = jnp.dot(y.astype(BF16), wo_ref[cs, :], preferred_element_type=F32)
        state["z"] = part if "z" not in state else state["z"] + part

    n_p = D_MODEL // MXU_DIM
    att = [functools.partial(branch_piece, p, wa_ref, lambda: ya_scr[1 - slot], "a") for p in range(n_p)]
    rnn = [functools.partial(branch_piece, p, wr_ref, lambda: yr_scr[...], "r") for p in range(n_p)]
    outs = [functools.partial(out_piece, p) for p in range(n_p)]
    fillers = [rnn_in] + att + rnn[:n_p - 1] + [outs[0], rnn[n_p - 1]] + outs[1:] if merge else []
    per_unit = [2, 2, 1, 2, 1, 2, 1, 2]
    assert sum(per_unit) == 3 * n_p + 1 and len(per_unit) == nqb * N_KV_HEADS
    if units:
        units[0][0]()
    for u, (_, softmax_values) in enumerate(units):
        if u + 1 < len(units):
            units[u + 1][0]()
        for f in fillers[:per_unit[u]]:
            f()
        fillers = fillers[per_unit[u]:]
        softmax_values()
    for f in fillers:
        f()

    if merge:
        x1 = x_ref[0] + mod_ref[0, 2:3, :] * state["z"]
        x1_ref[0] = x1
        ms = jnp.mean(x1 * x1, axis=-1, keepdims=True)
        n2 = x1 * lax.rsqrt(ms + EPS) * nw_ref[...]
        h2_ref[0] = (n2 * (1.0 + mod_ref[0, 4:5, :]) + mod_ref[0, 3:4, :]).astype(h2_ref.dtype)


def _mix_call(sink, q, k, v, kc, vc, hf, hb, rg, ga, gb, x, mod, w_o_rnn, w_o_attn, w_out, nw2, nqb):
    bsz, length, _ = x.shape
    clen = kc.shape[1]
    tm = nqb * BLOCK
    n_blk = length // BLOCK
    nb = length // tm
    n_steps = bsz * nb + 1
    att = lambda j: divmod(jnp.minimum(j, n_steps - 2), nb)
    mrg = lambda j: divmod(jnp.maximum(j - 1, 0), nb)
    main = lambda j: (att(j)[0], att(j)[1], 0)
    prev = lambda j: (att(j)[0], jnp.maximum(att(j)[1] * nqb - 1, 0), 0)
    nxt = lambda j: (att(j)[0], jnp.minimum((att(j)[1] + 1) * nqb, n_blk - 1), 0)
    kv_specs = [pl.BlockSpec((1, tm, KV_WIDTH), main),
                pl.BlockSpec((1, BLOCK, KV_WIDTH), prev),
                pl.BlockSpec((1, BLOCK, KV_WIDTH), nxt)]
    ctx_spec = pl.BlockSpec((1, clen, KV_WIDTH), lambda j: (att(j)[0], 0, 0))
    blk = pl.BlockSpec((1, tm, D_MODEL), lambda j: (mrg(j)[0], mrg(j)[1], 0))
    return pl.pallas_call(
        functools.partial(_mix_kernel, nqb=nqb, length=length, n_steps=n_steps),
        grid=(n_steps,),
        in_specs=[pl.BlockSpec(memory_space=pltpu.SMEM),
                  pl.BlockSpec((1, tm, ATTN_WIDTH), main)] + kv_specs + kv_specs + [ctx_spec, ctx_spec]
                 + [blk, blk, blk, blk, blk, blk,
                    pl.BlockSpec((1, N_MOD, D_MODEL), lambda j: (mrg(j)[0], 0, 0)),
                    _const_spec((D_RNN, D_MODEL)), _const_spec((ATTN_WIDTH, D_MODEL)),
                    _const_spec((D_MODEL, D_MODEL)), _const_spec((1, D_MODEL))],
        out_specs=[blk, blk],
        out_shape=[jax.ShapeDtypeStruct((bsz, length, D_MODEL), F32),
                   jax.ShapeDtypeStruct((bsz, length, D_MODEL), BF16)],
        scratch_shapes=[pltpu.VMEM((2, tm, ATTN_WIDTH), BF16),
                        pltpu.VMEM((tm, D_RNN), BF16),
                        pltpu.VMEM((2, Q_PER_KV * BLOCK, 3 * BLOCK + clen), F32),
                        pltpu.VMEM((tm + 2 * BLOCK, KV_WIDTH), BF16),
                        pltpu.VMEM((tm + 2 * BLOCK, 2 * KV_WIDTH), BF16),
                        pltpu.VMEM((clen, 2 * KV_WIDTH), BF16)],
        compiler_params=pltpu.CompilerParams(dimension_semantics=("arbitrary",),
                                             vmem_limit_bytes=VMEM_LIMIT),
        name="mix",
    )(sink, q, k, k, k, v, v, v, kc, vc, hf, hb, rg, ga, gb, x, mod, w_o_rnn, w_o_attn, w_out, nw2)


def _ffn_kernel(h_ref, hp_ref, hn_ref, x1_ref, mod_ref, wu_ref, cw_ref, cb_ref, wd_ref, fw_ref,
                o_ref, hbuf, ua_scr, uv_scr, g_scr, *, tm, nb, n_steps):
    j = pl.program_id(0)
    i = jnp.minimum(j, n_steps - 2) % nb
    ring = j % 2

    def stage_rows():
        hbuf[pl.ds(0, HALO), :] = jnp.where(i > 0, hp_ref[0], jnp.zeros_like(hp_ref[0]))
        hbuf[pl.ds(HALO, tm), :] = h_ref[0]
        hbuf[pl.ds(HALO + tm, HALO), :] = jnp.where(i < nb - 1, hn_ref[0], jnp.zeros_like(hn_ref[0]))

    rows = tm + 2 * HALO

    n_chunks = D_FF // FFN_CHUNK
    pieces = FFN_CHUNK // MXU_DIM
    lg_per_piece = MXU_DIM // LANES

    def up_piece(u_scr, c, p, c0):
        u = jnp.dot(hbuf[...], wu_ref[:, c0 + p * MXU_DIM:c0 + (p + 1) * MXU_DIM], preferred_element_type=F32)
        for q in range(lg_per_piece):
            u_scr[c % 2, p * lg_per_piece + q, pl.ds(0, rows, stride=2), :] = u[:, q * LANES:(q + 1) * LANES]

    def conv(u_scr, slot, lg, c0):
        cs = slice(c0 + lg * LANES, c0 + (lg + 1) * LANES)
        taps = [u_scr[slot, lg, pl.ds(2 * (HALO - FFN_CONV_W // 2 + k), tm, stride=2), :] for k in range(FFN_CONV_W)]
        return cb_ref[:, cs] + sum(cw_ref[k:k + 1, cs] * taps[k] for k in range(FFN_CONV_W))

    def act_piece(c, lg):
        ca, cv = c * FFN_CHUNK, D_FF + c * FFN_CHUNK
        g = _gelu(conv(ua_scr, c % 2, lg, ca)) * conv(uv_scr, c % 2, lg, cv)
        g_scr[ring, :, ca + lg * LANES:ca + (lg + 1) * LANES] = g.astype(BF16)

    half = D_MODEL // 2

    def run(activate, project):
        z = [None, None]

        def down_piece(c, hh):
            ks = slice(c * FFN_CHUNK, (c + 1) * FFN_CHUNK)
            part = jnp.dot(g_scr[1 - ring, :, ks], wd_ref[ks, hh * half:(hh + 1) * half],
                           preferred_element_type=F32)
            z[hh] = part if z[hh] is None else z[hh] + part

        downs = [functools.partial(down_piece, c, hh) for c in range(n_chunks) for hh in range(2)] if project else []
        n_down = [1] * (n_chunks - 1) + [len(downs)]
        if activate:
            stage_rows()
            for p in range(pieces):
                up_piece(ua_scr, 0, p, 0)
                up_piece(uv_scr, 0, p, D_FF)
        for c in range(n_chunks):
            mm, acts = [], []
            if activate and c + 1 < n_chunks:
                for p in range(pieces):
                    mm.append(functools.partial(up_piece, ua_scr, c + 1, p, (c + 1) * FFN_CHUNK))
                    mm.append(functools.partial(up_piece, uv_scr, c + 1, p, D_FF + (c + 1) * FFN_CHUNK))
            mm += downs[:n_down[c]]
            downs = downs[n_down[c]:]
            if activate:
                acts = [functools.partial(act_piece, c, lg) for lg in range(FFN_CHUNK // LANES)]
            while mm or acts:
                if mm:
                    mm.pop(0)()
                if acts:
                    acts.pop(0)()
        if project:
            x2 = x1_ref[0] + mod_ref[0, 5:6, :] * jnp.concatenate(z, axis=1)
            ms = jnp.mean(x2 * x2, axis=-1, keepdims=True)
            o_ref[0] = x2 * lax.rsqrt(ms + EPS) * fw_ref[...]

    @pl.when(j == 0)
    def _():
        g_scr[1] = jnp.zeros(g_scr.shape[1:], g_scr.dtype)

    run(True, True)


def _ffn_call(h2, x1, mod, w_up, conv_w, conv_b, w_down, fw, tm):
    bsz, length, _ = x1.shape
    nb = length // tm
    n_steps = bsz * nb + 1
    hb = tm // HALO
    n_hb = length // HALO
    act = lambda j: divmod(jnp.minimum(j, n_steps - 2), nb)
    dwn = lambda j: divmod(jnp.maximum(j - 1, 0), nb)
    tok_out = lambda j: (dwn(j)[0], dwn(j)[1], 0)
    return pl.pallas_call(
        functools.partial(_ffn_kernel, tm=tm, nb=nb, n_steps=n_steps),
        grid=(n_steps,),
        in_specs=[pl.BlockSpec((1, tm, D_MODEL), lambda j: (act(j)[0], act(j)[1], 0)),
                  pl.BlockSpec((1, HALO, D_MODEL), lambda j: (act(j)[0], jnp.maximum(act(j)[1] * hb - 1, 0), 0)),
                  pl.BlockSpec((1, HALO, D_MODEL),
                               lambda j: (act(j)[0], jnp.minimum((act(j)[1] + 1) * hb, n_hb - 1), 0)),
                  pl.BlockSpec((1, tm, D_MODEL), tok_out),
                  pl.BlockSpec((1, N_MOD, D_MODEL), lambda j: (dwn(j)[0], 0, 0)),
                  _const_spec((D_MODEL, 2 * D_FF)), _const_spec((FFN_CONV_W, 2 * D_FF)),
                  _const_spec((1, 2 * D_FF)), _const_spec((D_FF, D_MODEL)), _const_spec((1, D_MODEL))],
        out_specs=pl.BlockSpec((1, tm, D_MODEL), tok_out),
        out_shape=jax.ShapeDtypeStruct((bsz, length, D_MODEL), F32),
        scratch_shapes=[pltpu.VMEM((tm + 2 * HALO, D_MODEL), BF16),
                        pltpu.VMEM((2, FFN_CHUNK // LANES, 2 * (tm + 2 * HALO), LANES), F32),
                        pltpu.VMEM((2, FFN_CHUNK // LANES, 2 * (tm + 2 * HALO), LANES), F32),
                        pltpu.VMEM((2, tm, D_FF), BF16)],
        compiler_params=pltpu.CompilerParams(dimension_semantics=("arbitrary",),
                                             vmem_limit_bytes=VMEM_LIMIT),
        name="ffn",
    )(h2, h2, h2, x1, mod, w_up, conv_w, conv_b, w_down, fw)


def _rope_tables(length):
    pos = np.arange(length)
    row, col = pos // GRID_W, pos % GRID_W
    half = HEAD_DIM // 2
    inv = ROPE_THETA ** (-np.arange(0, half, 2, dtype=np.float64) / half)
    ar, ac = row[:, None] * inv, col[:, None] * inv
    cos = np.concatenate([np.cos(ar), np.cos(ar), np.cos(ac), np.cos(ac)], axis=1)
    sin = np.concatenate([-np.sin(ar), np.sin(ar), -np.sin(ac), np.sin(ac)], axis=1)
    return jnp.asarray(cos, F32), jnp.asarray(sin, F32)


def _gate_weights(w_a, w_x):
    per = GATE_TILE // LRU_BLOCK
    eye = jnp.eye(per, dtype=F32)

    def bd(w):
        w = w.reshape(N_DIRS, GATE_TILES, per, LRU_BLOCK, LRU_BLOCK)
        return jnp.einsum('dtpcn,pq->dtpcqn', w, eye).reshape(N_DIRS, GATE_TILES, GATE_TILE, GATE_TILE)

    return jnp.concatenate([bd(w_a), bd(w_x)], axis=-1).astype(BF16)


def kernel(x, c, ctx, c_ctx, w_mod, b_mod, norm1_w, w_in, lru_conv_w, lru_conv_b, gate_a_w, gate_a_b,
           gate_x_w, gate_x_b, lru_lambda, sink_logit, w_o_rnn, w_o_attn, w_out, norm2_w, w_up,
           ffn_conv_w, ffn_conv_b, w_down, final_norm_w):
    bsz, length, _ = x.shape
    l = 0

    cs = jnp.concatenate([c, c_ctx[None], jnp.zeros((SUBLANES - bsz - 1, D_MODEL), F32)], axis=0)
    mod = _mod_call(cs, w_mod[l], b_mod[l][None]).reshape(SUBLANES, N_MOD, D_MODEL)

    w_in_b = w_in[l].astype(BF16)
    nw1 = norm1_w[l][None]
    cw, cb = lru_conv_w[l], lru_conv_b[l][None]

    plan_x = [(COL_R, D_RNN, "conv"), (COL_RG, D_RNN, "gelu"), (COL_Q, ATTN_WIDTH, "rope_q"),
              (COL_K, KV_WIDTH, "rope_k"), (COL_V, KV_WIDTH, "bf16"), (COL_GA, D_MODEL, "sigmoid"),
              (COL_GB, D_MODEL, "sigmoid")]
    xc, rg, q, k, v, ga, gb = _inproj_call(x, mod, 0, nw1, w_in_b, cw, cb, _rope_tables(length), plan_x, tm=512)
    plan_c = [(COL_R, D_RNN, "conv"), (COL_K, KV_WIDTH, "bf16"), (COL_V, KV_WIDTH, "bf16")]
    xc_ctx, kc, vc = _inproj_call(ctx, mod, bsz, nw1, w_in_b, cw, cb, None, plan_c, tm=ctx.shape[1])

    wg = _gate_weights(gate_a_w[l], gate_x_w[l])
    bg = jnp.stack([gate_a_b[l], gate_x_b[l]], axis=1)
    hf, hb = _scan_call(xc, xc_ctx, wg, bg, lru_lambda[l][:, None, :], tm=256)

    x1, h2 = _mix_call(sink_logit[l], q, k, v, kc, vc, hf, hb, rg, ga, gb, x, mod, w_o_rnn[l].astype(BF16),
                       w_o_attn[l].astype(BF16), w_out[l].astype(BF16), norm2_w[l][None], nqb=4)
    return _ffn_call(h2, x1, mod, w_up[l].astype(BF16), ffn_conv_w[l], ffn_conv_b[l][None],
                     w_down[l].astype(BF16), final_norm_w[None], tm=512)
```

```python
import functools

import numpy as np
import jax
import jax.numpy as jnp
from jax import lax
from jax.experimental import pallas as pl
from jax.experimental.pallas import tpu as pltpu

F32 = jnp.float32
BF16 = jnp.bfloat16

D_MODEL = 1024
N_HEADS = 8
N_KV_HEADS = 2
HEAD_DIM = 128
Q_PER_KV = N_HEADS // N_KV_HEADS
ATTN_WIDTH = N_HEADS * HEAD_DIM
KV_WIDTH = N_KV_HEADS * HEAD_DIM
WINDOW = 128
BLOCK = 128
GRID_W = 64
ROPE_THETA = 10000.0
D_RNN = D_MODEL
N_LRU_BLOCKS = 16
LRU_BLOCK = D_RNN // N_LRU_BLOCKS
LRU_C = 8.0
LRU_CONV_W = 4
LRU_CONV_LEFT = 2
N_DIRS = 2
D_FF = 3 * D_MODEL
FFN_CONV_W = 3
N_MOD = 6
EPS = 1e-6
NEG_INF = -1e30

COL_R = 0
COL_RG = COL_R + D_RNN
COL_Q = COL_RG + D_RNN
COL_K = COL_Q + ATTN_WIDTH
COL_V = COL_K + KV_WIDTH
COL_GA = COL_V + KV_WIDTH
COL_GB = COL_GA + D_MODEL
D_IN = COL_GB + D_MODEL

SUBLANES = 8
BF16_SUBLANES = 16
LANES = 128
MXU_DIM = 256
VMEM_LIMIT = 56 * 1024 * 1024

LANE_GROUPS = D_RNN // LANES
GATE_TILE = MXU_DIM
GATE_TILES = D_RNN // GATE_TILE
FFN_CHUNK = 512
HALO = BF16_SUBLANES
LOG2E = float(np.log2(np.e))
assert BLOCK == LANES and WINDOW >= BLOCK - 1
TINY = 1e-30


GELU_K1 = float(np.sqrt(2.0 / np.pi))
GELU_K2 = GELU_K1 * 0.044715


def _gelu(v):
    return (0.5 * v) * (1.0 + jnp.tanh(v * (GELU_K1 + GELU_K2 * (v * v))))


def _sigmoid(v):
    return 0.5 * jnp.tanh(0.5 * v) + 0.5


def _const_spec(shape):
    nd = len(shape)
    return pl.BlockSpec(shape, lambda *_: (0,) * nd, pipeline_mode=pl.Buffered(1))


def _halo_specs(tm, length, width):
    hb = tm // HALO
    n_hb = length // HALO
    prev = pl.BlockSpec((1, HALO, width), lambda b, i: (b, jnp.maximum(i * hb - 1, 0), 0))
    nxt = pl.BlockSpec((1, HALO, width), lambda b, i: (b, jnp.minimum((i + 1) * hb, n_hb - 1), 0))
    return prev, nxt


def _mod_kernel(c_ref, w_ref, b_ref, o_ref):
    c = c_ref[...]
    s = c * _sigmoid(c)
    o_ref[...] = jnp.dot(s, w_ref[...], preferred_element_type=F32) + b_ref[...]


def _mod_call(cs, w_mod, b_mod):
    n_out = w_mod.shape[1]
    tn = n_out // 4
    return pl.pallas_call(
        _mod_kernel,
        grid=(n_out // tn,),
        in_specs=[pl.BlockSpec((SUBLANES, D_MODEL), lambda j: (0, 0)),
                  pl.BlockSpec((D_MODEL, tn), lambda j: (0, j)),
                  pl.BlockSpec((1, tn), lambda j: (0, j))],
        out_specs=pl.BlockSpec((SUBLANES, tn), lambda j: (0, j)),
        out_shape=jax.ShapeDtypeStruct((SUBLANES, n_out), F32),
        compiler_params=pltpu.CompilerParams(dimension_semantics=("arbitrary",),
                                             vmem_limit_bytes=VMEM_LIMIT),
        name="mod",
    )(cs, w_mod, b_mod)


def _inproj_kernel(*refs, plan, rope, tm, nb, n_steps):
    x_ref, xp_ref, xn_ref, mod_ref, nw_ref, w_ref, cw_ref, cb_ref = refs[:8]
    n_in = 10 if rope else 8
    out_refs = refs[n_in:n_in + len(plan)]
    h_scr, r_scr = refs[-2:]
    j = pl.program_id(0)
    i = jnp.minimum(j, n_steps - 2) % nb
    slot = j % 2
    h_in = h_scr.at[1 - slot]

    def norm_mod(x):
        ms = jnp.mean(x * x, axis=-1, keepdims=True)
        y = x * lax.rsqrt(ms + EPS) * nw_ref[...]
        return y * (1.0 + mod_ref[0, 1:2, :]) + mod_ref[0, 0:1, :]

    def norm_stage():
        h_scr[slot, pl.ds(0, HALO), :] = jnp.where(i > 0, norm_mod(xp_ref[0]), 0.0).astype(BF16)
        h_scr[slot, pl.ds(HALO, tm), :] = norm_mod(x_ref[0]).astype(BF16)
        h_scr[slot, pl.ds(HALO + tm, HALO), :] = jnp.where(i < nb - 1, norm_mod(xn_ref[0]), 0.0).astype(BF16)

    def piece(c0, kind, o_ref, p):
        cs = slice(p * MXU_DIM, (p + 1) * MXU_DIM)
        wc = slice(c0 + p * MXU_DIM, c0 + (p + 1) * MXU_DIM)
        if kind == "conv":
            r = jnp.dot(h_in[...], w_ref[:, wc], preferred_element_type=F32)
            for q in range(MXU_DIM // LANES):
                lg = p * (MXU_DIM // LANES) + q
                ls = slice(lg * LANES, (lg + 1) * LANES)
                r_scr[lg, pl.ds(0, tm + 2 * HALO, stride=2), :] = r[:, q * LANES:(q + 1) * LANES]
                taps = [r_scr[lg, pl.ds(2 * (HALO - LRU_CONV_LEFT + k), tm, stride=2), :] for k in range(LRU_CONV_W)]
                o_ref[0, :, ls] = 0.5 * cb_ref[:, ls] + sum((0.5 * cw_ref[k:k + 1, ls]) * taps[k]
                                                            for k in range(LRU_CONV_W))
            return
        res = jnp.dot(h_in[pl.ds(HALO, tm), :], w_ref[:, wc], preferred_element_type=F32)
        if kind in ("rope_q", "rope_k"):
            scale = HEAD_DIM ** -0.5 * LOG2E if kind == "rope_q" else None
            cos, sin = refs[8][...], refs[9][...]
            lane = lax.broadcasted_iota(jnp.int32, (tm, HEAD_DIM), 1)
            low = (lane % (HEAD_DIM // 2)) < (HEAD_DIM // 4)
            for hh in range(MXU_DIM // HEAD_DIM):
                v = res[:, hh * HEAD_DIM:(hh + 1) * HEAD_DIM]
                partner = jnp.where(low, pltpu.roll(v, HEAD_DIM - HEAD_DIM // 4, 1), pltpu.roll(v, HEAD_DIM // 4, 1))
                rot = v * cos + partner * sin
                if scale is not None:
                    rot = rot * scale
                o_ref[0, :, p * MXU_DIM + hh * HEAD_DIM:p * MXU_DIM + (hh + 1) * HEAD_DIM] = rot.astype(o_ref.dtype)
        else:
            act = {"gelu": _gelu, "sigmoid": _sigmoid, "bf16": lambda t: t}[kind]
            o_ref[0, :, cs] = act(res).astype(o_ref.dtype)

    def project():
        by_kind = {}
        for (c0, width, kind), o_ref in zip(plan, out_refs):
            by_kind.setdefault(kind, []).extend(
                functools.partial(piece, c0, kind, o_ref, p) for p in range(width // MXU_DIM))
        take = lambda kind: by_kind[kind].pop(0)() if by_kind.get(kind) else None
        take("conv"), take("sigmoid")
        norm_stage()
        take("sigmoid"), take("bf16")
        while by_kind.get("conv"):
            take("conv"), take("sigmoid"), take("sigmoid"), take("bf16")
        while by_kind.get("rope_q") or by_kind.get("rope_k"):
            take("rope_q"), take("gelu"), take("rope_k"), take("bf16")
        for kind in list(by_kind):
            while by_kind[kind]:
                take(kind)

    pl.when(j == 0)(norm_stage)
    pl.when(j > 0)(project)


def _inproj_call(x, mod, mod_row0, nw, w_in, conv_w, conv_b, tables, plan, tm):
    bsz, length, _ = x.shape
    rope = tables is not None
    nb = length // tm
    n_steps = bsz * nb + 1
    hb = tm // HALO
    n_hb = length // HALO
    nrm = lambda j: divmod(jnp.minimum(j, n_steps - 2), nb)
    prj = lambda j: divmod(jnp.maximum(j - 1, 0), nb)
    tok_out = lambda j: (prj(j)[0], prj(j)[1], 0)
    in_specs = [pl.BlockSpec((1, tm, D_MODEL), lambda j: (nrm(j)[0], nrm(j)[1], 0)),
                pl.BlockSpec((1, HALO, D_MODEL), lambda j: (nrm(j)[0], jnp.maximum(nrm(j)[1] * hb - 1, 0), 0)),
                pl.BlockSpec((1, HALO, D_MODEL),
                             lambda j: (nrm(j)[0], jnp.minimum((nrm(j)[1] + 1) * hb, n_hb - 1), 0)),
                pl.BlockSpec((1, N_MOD, D_MODEL), lambda j: (nrm(j)[0] * (mod_row0 == 0) + mod_row0, 0, 0)),
                _const_spec((1, D_MODEL)),
                _const_spec((D_MODEL, D_IN)),
                _const_spec((LRU_CONV_W, D_RNN)),
                _const_spec((1, D_RNN))]
    args = [x, x, x, mod, nw, w_in, conv_w, conv_b]
    if rope:
        in_specs += [pl.BlockSpec((tm, HEAD_DIM), lambda j: (prj(j)[1], 0))] * 2
        args += list(tables)
    out_specs, out_shape = [], []
    for (_, width, kind) in plan:
        dt = F32 if kind == "conv" else BF16
        out_specs.append(pl.BlockSpec((1, tm, width), tok_out))
        out_shape.append(jax.ShapeDtypeStruct((bsz, length, width), dt))
    return pl.pallas_call(
        functools.partial(_inproj_kernel, plan=tuple(plan), rope=rope, tm=tm, nb=nb, n_steps=n_steps),
        grid=(n_steps,),
        in_specs=in_specs, out_specs=out_specs, out_shape=out_shape,
        scratch_shapes=[pltpu.VMEM((2, tm + 2 * HALO, D_MODEL), BF16),
                        pltpu.VMEM((D_RNN // LANES, 2 * (tm + 2 * HALO), LANES), F32)],
        compiler_params=pltpu.CompilerParams(dimension_semantics=("arbitrary",),
                                             vmem_limit_bytes=VMEM_LIMIT),
        name="inproj_x" if rope else "inproj_ctx",
    )(*args)


def _scan_kernel(xf_ref, xb_ref, xc_ref, wg_ref, bg_ref, lam_ref, hf_ref, hb_ref,
                 a3, b3, h3, hc_scr, *, tm, nb, clen, bsz):
    i = pl.program_id(0)
    chains = [(d, b) for d in range(N_DIRS) for b in range(bsz)]
    lg_per_tile = GATE_TILE // LANES

    def gate_piece(src_ref, ws, d, b, ch, kt, n):
        lam = lam_ref[d]
        softplus = jnp.maximum(-lam, 0.0) + jnp.log(1.0 + jnp.exp(-jnp.abs(lam)))
        hd2 = (-0.5 * LRU_C * LOG2E) * softplus
        cs = slice(kt * GATE_TILE, (kt + 1) * GATE_TILE)
        xk = src_ref[b, pl.ds(0, n), cs]
        g = jnp.dot(xk.astype(BF16), wg_ref[d, kt], preferred_element_type=F32)
        t_r = jnp.tanh(g[:, :GATE_TILE] + 0.5 * bg_ref[d, 0:1, cs])
        t_i = jnp.tanh(g[:, GATE_TILE:] + 0.5 * bg_ref[d, 1:2, cs])
        a = jnp.exp2(hd2[:, cs] * t_r + hd2[:, cs])
        y = 1.0 - a * a
        bb = (y * lax.rsqrt(jnp.maximum(y, TINY)) * xk) * (t_i + 1.0)
        for gg in range(lg_per_tile):
            lg = kt * lg_per_tile + gg
            ls = slice(gg * LANES, (gg + 1) * LANES)
            a3[ws, ch, pl.ds(0, n // SUBLANES), pl.ds(lg, SUBLANES, stride=LANE_GROUPS), :] = (
                a[:, ls].reshape(n // SUBLANES, SUBLANES, LANES))
            b3[ws, ch, pl.ds(0, n // SUBLANES), pl.ds(lg, SUBLANES, stride=LANE_GROUPS), :] = (
                bb[:, ls].reshape(n // SUBLANES, SUBLANES, LANES))

    def scan_group(rs, n8, tau, hs, keep):
        for rho in range(SUBLANES):
            for ch, (d, b) in enumerate(chains):
                tt, rr = (tau, rho) if d == 0 else (n8 - 1 - tau, SUBLANES - 1 - rho)
                win = pl.ds(rr * LANE_GROUPS, LANE_GROUPS)
                h = a3[rs, ch, tt, win, :] * hs[ch] + b3[rs, ch, tt, win, :]
                hs[ch] = h
                if keep:
                    h3[ch, tt, win, :] = h

    def emit(rows):
        r0, r1 = rows
        for ch, (d, b) in enumerate(chains):
            lo, hi = (r0, r1) if d == 0 else (tm - r1, tm - r0)
            o_ref = hf_ref if d == 0 else hb_ref
            for lg in range(LANE_GROUPS):
                slab = h3[ch, pl.ds(lo // SUBLANES, (hi - lo) // SUBLANES), pl.ds(lg, SUBLANES, stride=LANE_GROUPS), :]
                o_ref[b, pl.ds(lo, hi - lo), lg * LANES:(lg + 1) * LANES] = (
                    slab.reshape(hi - lo, LANES).astype(o_ref.dtype))

    @pl.when(i == 0)
    def _():
        for ch, (d, b) in enumerate(chains):
            for kt in range(GATE_TILES):
                gate_piece(xc_ref, 1, d, b, ch, kt, clen)

        def body(tau, hs):
            hs = list(hs)
            scan_group(1, clen // SUBLANES, tau, hs, False)
            return tuple(hs)

        hs = lax.fori_loop(0, clen // SUBLANES, body, (jnp.zeros((LANE_GROUPS, LANES), F32),) * len(chains))
        for ch in range(len(chains)):
            hc_scr[ch] = hs[ch]

    def step(ws, rs, gate, scan):
        n8 = tm // SUBLANES
        pieces = [(d, b, ch, kt) for ch, (d, b) in enumerate(chains) for kt in range(GATE_TILES)]
        per_piece = n8 // len(pieces)
        hs = [hc_scr[ch] for ch in range(len(chains))] if scan else None
        tau = 0
        for (d, b, ch, kt) in pieces:
            if gate:
                gate_piece(xf_ref if d == 0 else xb_ref, ws, d, b, ch, kt, tm)
            for _ in range(per_piece if scan else 0):
                scan_group(rs, n8, tau, hs, True)
                tau += 1
                if tau == n8 // 2:
                    emit((0, tm // 2))
        if scan:
            emit((tm // 2, tm))
            for ch in range(len(chains)):
                hc_scr[ch] = hs[ch]

    pl.when(i == 0)(functools.partial(step, 0, 1, True, False))
    for parity in range(2):
        pl.when((i > 0) & (i < nb) & (i % 2 == parity))(functools.partial(step, parity, 1 - parity, True, True))
    pl.when(i == nb)(functools.partial(step, nb % 2, 1 - nb % 2, False, True))


def _scan_call(xc, xc_ctx, wg, bg, lam, tm):
    bsz, length, ch = xc.shape
    clen = xc_ctx.shape[1]
    nb = length // tm
    n_chains = N_DIRS * bsz
    assert clen <= tm and (tm // SUBLANES) % (n_chains * GATE_TILES) == 0
    rows8 = tm // SUBLANES
    slab = (n_chains, rows8, LANE_GROUPS * SUBLANES, LANES)
    gi = lambda i: jnp.minimum(i, nb - 1)
    si = lambda i: jnp.maximum(i - 1, 0)
    return pl.pallas_call(
        functools.partial(_scan_kernel, tm=tm, nb=nb, clen=clen, bsz=bsz),
        grid=(nb + 1,),
        in_specs=[pl.BlockSpec((bsz, tm, ch), lambda i: (0, gi(i), 0)),
                  pl.BlockSpec((bsz, tm, ch), lambda i: (0, nb - 1 - gi(i), 0)),
                  _const_spec((bsz, clen, ch)),
                  _const_spec((N_DIRS, GATE_TILES, GATE_TILE, 2 * GATE_TILE)),
                  _const_spec((N_DIRS, 2, ch)),
                  _const_spec((N_DIRS, 1, ch))],
        out_specs=[pl.BlockSpec((bsz, tm, ch), lambda i: (0, si(i), 0)),
                   pl.BlockSpec((bsz, tm, ch), lambda i: (0, nb - 1 - si(i), 0))],
        out_shape=[jax.ShapeDtypeStruct((bsz, length, ch), BF16)] * 2,
        scratch_shapes=[pltpu.VMEM((2,) + slab, F32), pltpu.VMEM((2,) + slab, F32), pltpu.VMEM(slab, F32),
                        pltpu.VMEM((n_chains, LANE_GROUPS, LANES), F32)],
        compiler_params=pltpu.CompilerParams(dimension_semantics=("arbitrary",),
                                             vmem_limit_bytes=VMEM_LIMIT),
        name="scan",
    )(xc, xc, xc_ctx, wg, bg, lam)


def _attention(i, sink_ref, q_ref, k_ref, kp_ref, kn_ref, v_ref, vp_ref, vn_ref, kc_ref, vc_ref,
               ya_scr, s_scr, kcat, vcat, vctx, *, nqb, length):
    tq = nqb * BLOCK
    clen = kc_ref.shape[1]
    kcat[pl.ds(0, BLOCK), :] = kp_ref[0]
    kcat[pl.ds(BLOCK, tq), :] = k_ref[0]
    kcat[pl.ds(BLOCK + tq, BLOCK), :] = kn_ref[0]
    for g in range(N_KV_HEADS):
        gs = slice(g * HEAD_DIM, (g + 1) * HEAD_DIM)
        vs = slice(2 * g * HEAD_DIM, (2 * g + 1) * HEAD_DIM)
        os_ = slice((2 * g + 1) * HEAD_DIM, (2 * g + 2) * HEAD_DIM)
        vcat[pl.ds(0, BLOCK), vs] = vp_ref[0, :, gs]
        vcat[pl.ds(BLOCK, tq), vs] = v_ref[0, :, gs]
        vcat[pl.ds(BLOCK + tq, BLOCK), vs] = vn_ref[0, :, gs]
        vcat[:, os_] = jnp.ones((tq + 2 * BLOCK, HEAD_DIM), BF16)
        vctx[:, vs] = vc_ref[0, :, gs]
        vctx[:, os_] = jnp.ones((clen, HEAD_DIM), BF16)

    qi = lax.broadcasted_iota(jnp.int32, (BLOCK, 3 * BLOCK), 0)
    kj = lax.broadcasted_iota(jnp.int32, (BLOCK, 3 * BLOCK), 1)
    rel = kj - qi
    band = (rel >= BLOCK - WINDOW) & (rel <= BLOCK + WINDOW)
    nt = (((1,), (1,)), ((), ()))
    rows = Q_PER_KV * BLOCK
    masks = []
    for jq in range(nqb):
        kpos = (i * nqb + jq - 1) * BLOCK + kj
        masks.append((band & (kpos >= 0) & (kpos < length))[None])

    n_loc = 3 * BLOCK

    def scores(jq, g, buf):
        gs = slice(g * HEAD_DIM, (g + 1) * HEAD_DIM)
        q4 = jnp.concatenate(
            [q_ref[0, pl.ds(jq * BLOCK, BLOCK), h * HEAD_DIM:(h + 1) * HEAD_DIM]
             for h in range(g * Q_PER_KV, (g + 1) * Q_PER_KV)], axis=0)
        s_scr[buf, :, 0:n_loc] = lax.dot_general(q4, kcat[pl.ds(jq * BLOCK, n_loc), gs], nt,
                                                 preferred_element_type=F32)
        s_scr[buf, :, n_loc:n_loc + clen] = lax.dot_general(q4, kc_ref[0, :, gs], nt, preferred_element_type=F32)

    def softmax_values(jq, g, buf):
        vos = slice(2 * g * HEAD_DIM, (2 * g + 2) * HEAD_DIM)
        heads = [g * Q_PER_KV + hh for hh in range(Q_PER_KV)]
        sink = jnp.concatenate([jnp.full((BLOCK, 1), sink_ref[h] * LOG2E, F32) for h in heads], axis=0)
        cols = []
        for c in range(n_loc // BLOCK):
            s = s_scr[buf, :, c * BLOCK:(c + 1) * BLOCK]
            if c != 1:
                s = jnp.where(masks[jq][:, :, c * BLOCK:(c + 1) * BLOCK], s.reshape(Q_PER_KV, BLOCK, BLOCK),
                              NEG_INF).reshape(rows, BLOCK)
            cols.append(s)
        cols += [s_scr[buf, :, n_loc + c * LANES:n_loc + (c + 1) * LANES] for c in range(clen // LANES)]
        m = jnp.maximum(jnp.max(functools.reduce(jnp.maximum, cols), axis=-1, keepdims=True), sink)
        p = [jnp.exp2(s - m).astype(BF16) for s in cols]
        p_loc = jnp.concatenate(p[:n_loc // BLOCK], axis=1)
        p_ctx = jnp.concatenate(p[n_loc // BLOCK:], axis=1)
        o2 = (jnp.dot(p_loc, vcat[pl.ds(jq * BLOCK, n_loc), vos], preferred_element_type=F32)
              + jnp.dot(p_ctx, vctx[:, vos], preferred_element_type=F32))
        denom = o2[:, HEAD_DIM:] + jnp.exp2(sink - m)
        o = o2[:, :HEAD_DIM] * (1.0 / denom)
        for hh, h in enumerate(heads):
            ya_scr[pl.ds(jq * BLOCK, BLOCK), h * HEAD_DIM:(h + 1) * HEAD_DIM] = (
                o[hh * BLOCK:(hh + 1) * BLOCK].astype(ya_scr.dtype))

    order = [(jq, g) for jq in range(nqb) for g in range(N_KV_HEADS)]
    return [(functools.partial(scores, jq, g, u % 2), functools.partial(softmax_values, jq, g, u % 2))
            for u, (jq, g) in enumerate(order)]


def _mix_kernel(*refs, nqb, length, n_steps):
    j = pl.program_id(0)
    run = functools.partial(_mix_streams, refs, j, j % 2, nqb=nqb, length=length, n_steps=n_steps)
    ya_scr = refs[-6]

    @pl.when(j == 0)
    def _():
        ya_scr[1] = jnp.zeros(ya_scr.shape[1:], ya_scr.dtype)

    run(True, True)


def _mix_streams(refs, j, slot, attend, merge, *, nqb, length, n_steps):
    (sink_ref, q_ref, k_ref, kp_ref, kn_ref, v_ref, vp_ref, vn_ref, kc_ref, vc_ref,
     hf_ref, hb_ref, rg_ref, ga_ref, gb_ref, x_ref, mod_ref, wr_ref, wa_ref, wo_ref, nw_ref,
     x1_ref, h2_ref, ya_scr, yr_scr, s_scr, kcat, vcat, vctx) = refs
    nb = length // (nqb * BLOCK)
    units = []
    if attend:
        units = _attention(jnp.minimum(j, n_steps - 2) % nb, sink_ref, q_ref, k_ref, kp_ref, kn_ref, v_ref, vp_ref,
                           vn_ref, kc_ref, vc_ref, ya_scr.at[slot], s_scr, kcat, vcat, vctx, nqb=nqb,
                           length=length)

    state = {}

    def rnn_in():
        y_rnn = (hf_ref[0].astype(F32) + hb_ref[0].astype(F32)) * rg_ref[0].astype(F32)
        yr_scr[...] = y_rnn.astype(BF16)

    def branch_piece(p, w_ref, src, key):
        cs = slice(p * MXU_DIM, (p + 1) * MXU_DIM)
        state[key, p] = jnp.dot(src(), w_ref[:, cs], preferred_element_type=F32)

    def out_piece(p):
        cs = slice(p * MXU_DIM, (p + 1) * MXU_DIM)
        y = (ga_ref[0, :, cs].astype(F32) * state.pop(("r", p))
             + gb_ref[0, :, cs].astype(F32) * state.pop(("a", p)))
        part = jnp.dot(y.astype(BF16), wo_ref[cs, :], preferred_element_type=F32)
        state["z"] = part if "z" not in state else state["z"] + part

    n_p = D_MODEL // MXU_DIM
    att = [functools.partial(branch_piece, p, wa_ref, lambda: ya_scr[1 - slot], "a") for p in range(n_p)]
    rnn = [functools.partial(branch_piece, p, wr_ref, lambda: yr_scr[...], "r") for p in range(n_p)]
    outs = [functools.partial(out_piece, p) for p in range(n_p)]
    fillers = [rnn_in] + att + rnn[:n_p - 1] + [outs[0], rnn[n_p - 1]] + outs[1:] if merge else []
    per_unit = [2, 2, 1, 2, 1, 2, 1, 2]
    assert sum(per_unit) == 3 * n_p + 1 and len(per_unit) == nqb * N_KV_HEADS
    if units:
        units[0][0]()
    for u, (_, softmax_values) in enumerate(units):
        if u + 1 < len(units):
            units[u + 1][0]()
        for f in fillers[:per_unit[u]]:
            f()
        fillers = fillers[per_unit[u]:]
        softmax_values()
    for f in fillers:
        f()

    if merge:
        x1 = x_ref[0] + mod_ref[0, 2:3, :] * state["z"]
        x1_ref[0] = x1
        ms = jnp.mean(x1 * x1, axis=-1, keepdims=True)
        n2 = x1 * lax.rsqrt(ms + EPS) * nw_ref[...]
        h2_ref[0] = (n2 * (1.0 + mod_ref[0, 4:5, :]) + mod_ref[0, 3:4, :]).astype(h2_ref.dtype)


def _mix_call(sink, q, k, v, kc, vc, hf, hb, rg, ga, gb, x, mod, w_o_rnn, w_o_attn, w_out, nw2, nqb):
    bsz, length, _ = x.shape
    clen = kc.shape[1]
    tm = nqb * BLOCK
    n_blk = length // BLOCK
    nb = length // tm
    n_steps = bsz * nb + 1
    att = lambda j: divmod(jnp.minimum(j, n_steps - 2), nb)
    mrg = lambda j: divmod(jnp.maximum(j - 1, 0), nb)
    main = lambda j: (att(j)[0], att(j)[1], 0)
    prev = lambda j: (att(j)[0], jnp.maximum(att(j)[1] * nqb - 1, 0), 0)
    nxt = lambda j: (att(j)[0], jnp.minimum((att(j)[1] + 1) * nqb, n_blk - 1), 0)
    kv_specs = [pl.BlockSpec((1, tm, KV_WIDTH), main),
                pl.BlockSpec((1, BLOCK, KV_WIDTH), prev),
                pl.BlockSpec((1, BLOCK, KV_WIDTH), nxt)]
    ctx_spec = pl.BlockSpec((1, clen, KV_WIDTH), lambda j: (att(j)[0], 0, 0))
    blk = pl.BlockSpec((1, tm, D_MODEL), lambda j: (mrg(j)[0], mrg(j)[1], 0))
    return pl.pallas_call(
        functools.partial(_mix_kernel, nqb=nqb, length=length, n_steps=n_steps),
        grid=(n_steps,),
        in_specs=[pl.BlockSpec(memory_space=pltpu.SMEM),
                  pl.BlockSpec((1, tm, ATTN_WIDTH), main)] + kv_specs + kv_specs + [ctx_spec, ctx_spec]
                 + [blk, blk, blk, blk, blk, blk,
                    pl.BlockSpec((1, N_MOD, D_MODEL), lambda j: (mrg(j)[0], 0, 0)),
                    _const_spec((D_RNN, D_MODEL)), _const_spec((ATTN_WIDTH, D_MODEL)),
                    _const_spec((D_MODEL, D_MODEL)), _const_spec((1, D_MODEL))],
        out_specs=[blk, blk],
        out_shape=[jax.ShapeDtypeStruct((bsz, length, D_MODEL), F32),
                   jax.ShapeDtypeStruct((bsz, length, D_MODEL), BF16)],
        scratch_shapes=[pltpu.VMEM((2, tm, ATTN_WIDTH), BF16),
                        pltpu.VMEM((tm, D_RNN), BF16),
                        pltpu.VMEM((2, Q_PER_KV * BLOCK, 3 * BLOCK + clen), F32),
                        pltpu.VMEM((tm + 2 * BLOCK, KV_WIDTH), BF16),
                        pltpu.VMEM((tm + 2 * BLOCK, 2 * KV_WIDTH), BF16),
                        pltpu.VMEM((clen, 2 * KV_WIDTH), BF16)],
        compiler_params=pltpu.CompilerParams(dimension_semantics=("arbitrary",),
                                             vmem_limit_bytes=VMEM_LIMIT),
        name="mix",
    )(sink, q, k, k, k, v, v, v, kc, vc, hf, hb, rg, ga, gb, x, mod, w_o_rnn, w_o_attn, w_out, nw2)


def _ffn_kernel(h_ref, hp_ref, hn_ref, x1_ref, mod_ref, wu_ref, cw_ref, cb_ref, wd_ref, fw_ref,
                o_ref, hbuf, ua_scr, uv_scr, g_scr, *, tm, nb, n_steps):
    j = pl.program_id(0)
    i = jnp.minimum(j, n_steps - 2) % nb
    ring = j % 2

    def stage_rows():
        hbuf[pl.ds(0, HALO), :] = jnp.where(i > 0, hp_ref[0], jnp.zeros_like(hp_ref[0]))
        hbuf[pl.ds(HALO, tm), :] = h_ref[0]
        hbuf[pl.ds(HALO + tm, HALO), :] = jnp.where(i < nb - 1, hn_ref[0], jnp.zeros_like(hn_ref[0]))

    rows = tm + 2 * HALO

    n_chunks = D_FF // FFN_CHUNK
    pieces = FFN_CHUNK // MXU_DIM
    lg_per_piece = MXU_DIM // LANES

    def up_piece(u_scr, c, p, c0):
        u = jnp.dot(hbuf[...], wu_ref[:, c0 + p * MXU_DIM:c0 + (p + 1) * MXU_DIM], preferred_element_type=F32)
        for q in range(lg_per_piece):
            u_scr[c % 2, p * lg_per_piece + q, pl.ds(0, rows, stride=2), :] = u[:, q * LANES:(q + 1) * LANES]

    def conv(u_scr, slot, lg, c0):
        cs = slice(c0 + lg * LANES, c0 + (lg + 1) * LANES)
        taps = [u_scr[slot, lg, pl.ds(2 * (HALO - FFN_CONV_W // 2 + k), tm, stride=2), :] for k in range(FFN_CONV_W)]
        return cb_ref[:, cs] + sum(cw_ref[k:k + 1, cs] * taps[k] for k in range(FFN_CONV_W))

    def act_piece(c, lg):
        ca, cv = c * FFN_CHUNK, D_FF + c * FFN_CHUNK
        g = _gelu(conv(ua_scr, c % 2, lg, ca)) * conv(uv_scr, c % 2, lg, cv)
        g_scr[ring, :, ca + lg * LANES:ca + (lg + 1) * LANES] = g.astype(BF16)

    half = D_MODEL // 2

    def run(activate, project):
        z = [None, None]

        def down_piece(c, hh):
            ks = slice(c * FFN_CHUNK, (c + 1) * FFN_CHUNK)
            part = jnp.dot(g_scr[1 - ring, :, ks], wd_ref[ks, hh * half:(hh + 1) * half],
                           preferred_element_type=F32)
            z[hh] = part if z[hh] is None else z[hh] + part

        downs = [functools.partial(down_piece, c, hh) for c in range(n_chunks) for hh in range(2)] if project else []
        n_down = [1] * (n_chunks - 1) + [len(downs)]
        if activate:
            stage_rows()
            for p in range(pieces):
                up_piece(ua_scr, 0, p, 0)
                up_piece(uv_scr, 0, p, D_FF)
        for c in range(n_chunks):
            mm, acts = [], []
            if activate and c + 1 < n_chunks:
                for p in range(pieces):
                    mm.append(functools.partial(up_piece, ua_scr, c + 1, p, (c + 1) * FFN_CHUNK))
                    mm.append(functools.partial(up_piece, uv_scr, c + 1, p, D_FF + (c + 1) * FFN_CHUNK))
            mm += downs[:n_down[c]]
            downs = downs[n_down[c]:]
            if activate:
                acts = [functools.partial(act_piece, c, lg) for lg in range(FFN_CHUNK // LANES)]
            while mm or acts:
                if mm:
                    mm.pop(0)()
                if acts:
                    acts.pop(0)()
        if project:
            x2 = x1_ref[0] + mod_ref[0, 5:6, :] * jnp.concatenate(z, axis=1)
            ms = jnp.mean(x2 * x2, axis=-1, keepdims=True)
            o_ref[0] = x2 * lax.rsqrt(ms + EPS) * fw_ref[...]

    @pl.when(j == 0)
    def _():
        g_scr[1] = jnp.zeros(g_scr.shape[1:], g_scr.dtype)

    run(True, True)


def _ffn_call(h2, x1, mod, w_up, conv_w, conv_b, w_down, fw, tm):
    bsz, length, _ = x1.shape
    nb = length // tm
    n_steps = bsz * nb + 1
    hb = tm // HALO
    n_hb = length // HALO
    act = lambda j: divmod(jnp.minimum(j, n_steps - 2), nb)
    dwn = lambda j: divmod(jnp.maximum(j - 1, 0), nb)
    tok_out = lambda j: (dwn(j)[0], dwn(j)[1], 0)
    return pl.pallas_call(
        functools.partial(_ffn_kernel, tm=tm, nb=nb, n_steps=n_steps),
        grid=(n_steps,),
        in_specs=[pl.BlockSpec((1, tm, D_MODEL), lambda j: (act(j)[0], act(j)[1], 0)),
                  pl.BlockSpec((1, HALO, D_MODEL), lambda j: (act(j)[0], jnp.maximum(act(j)[1] * hb - 1, 0), 0)),
                  pl.BlockSpec((1, HALO, D_MODEL),
                               lambda j: (act(j)[0], jnp.minimum((act(j)[1] + 1) * hb, n_hb - 1), 0)),
                  pl.BlockSpec((1, tm, D_MODEL), tok_out),
                  pl.BlockSpec((1, N_MOD, D_MODEL), lambda j: (dwn(j)[0], 0, 0)),
                  _const_spec((D_MODEL, 2 * D_FF)), _const_spec((FFN_CONV_W, 2 * D_FF)),
                  _const_spec((1, 2 * D_FF)), _const_spec((D_FF, D_MODEL)), _const_spec((1, D_MODEL))],
        out_specs=pl.BlockSpec((1, tm, D_MODEL), tok_out),
        out_shape=jax.ShapeDtypeStruct((bsz, length, D_MODEL), F32),
        scratch_shapes=[pltpu.VMEM((tm + 2 * HALO, D_MODEL), BF16),
                        pltpu.VMEM((2, FFN_CHUNK // LANES, 2 * (tm + 2 * HALO), LANES), F32),
                        pltpu.VMEM((2, FFN_CHUNK // LANES, 2 * (tm + 2 * HALO), LANES), F32),
                        pltpu.VMEM((2, tm, D_FF), BF16)],
        compiler_params=pltpu.CompilerParams(dimension_semantics=("arbitrary",),
                                             vmem_limit_bytes=VMEM_LIMIT),
        name="ffn",
    )(h2, h2, h2, x1, mod, w_up, conv_w, conv_b, w_down, fw)


def _rope_tables(length):
    pos = np.arange(length)
    row, col = pos // GRID_W, pos % GRID_W
    half = HEAD_DIM // 2
    inv = ROPE_THETA ** (-np.arange(0, half, 2, dtype=np.float64) / half)
    ar, ac = row[:, None] * inv, col[:, None] * inv
    cos = np.concatenate([np.cos(ar), np.cos(ar), np.cos(ac), np.cos(ac)], axis=1)
    sin = np.concatenate([-np.sin(ar), np.sin(ar), -np.sin(ac), np.sin(ac)], axis=1)
    return jnp.asarray(cos, F32), jnp.asarray(sin, F32)


def _gate_weights(w_a, w_x):
    per = GATE_TILE // LRU_BLOCK
    eye = jnp.eye(per, dtype=F32)

    def bd(w):
        w = w.reshape(N_DIRS, GATE_TILES, per, LRU_BLOCK, LRU_BLOCK)
        return jnp.einsum('dtpcn,pq->dtpcqn', w, eye).reshape(N_DIRS, GATE_TILES, GATE_TILE, GATE_TILE)

    return jnp.concatenate([bd(w_a), bd(w_x)], axis=-1).astype(BF16)


def kernel(x, c, ctx, c_ctx, w_mod, b_mod, norm1_w, w_in, lru_conv_w, lru_conv_b, gate_a_w, gate_a_b,
           gate_x_w, gate_x_b, lru_lambda, sink_logit, w_o_rnn, w_o_attn, w_out, norm2_w, w_up,
           ffn_conv_w, ffn_conv_b, w_down, final_norm_w):
    bsz, length, _ = x.shape
    l = 0

    cs = jnp.concatenate([c, c_ctx[None], jnp.zeros((SUBLANES - bsz - 1, D_MODEL), F32)], axis=0)
    mod = _mod_call(cs, w_mod[l], b_mod[l][None]).reshape(SUBLANES, N_MOD, D_MODEL)

    w_in_b = w_in[l].astype(BF16)
    nw1 = norm1_w[l][None]
    cw, cb = lru_conv_w[l], lru_conv_b[l][None]

    plan_x = [(COL_R, D_RNN, "conv"), (COL_RG, D_RNN, "gelu"), (COL_Q, ATTN_WIDTH, "rope_q"),
              (COL_K, KV_WIDTH, "rope_k"), (COL_V, KV_WIDTH, "bf16"), (COL_GA, D_MODEL, "sigmoid"),
              (COL_GB, D_MODEL, "sigmoid")]
    xc, rg, q, k, v, ga, gb = _inproj_call(x, mod, 0, nw1, w_in_b, cw, cb, _rope_tables(length), plan_x, tm=512)
    plan_c = [(COL_R, D_RNN, "conv"), (COL_K, KV_WIDTH, "bf16"), (COL_V, KV_WIDTH, "bf16")]
    xc_ctx, kc, vc = _inproj_call(ctx, mod, bsz, nw1, w_in_b, cw, cb, None, plan_c, tm=ctx.shape[1])

    wg = _gate_weights(gate_a_w[l], gate_x_w[l])
    bg = jnp.stack([gate_a_b[l], gate_x_b[l]], axis=1)
    hf, hb = _scan_call(xc, xc_ctx, wg, bg, lru_lambda[l][:, None, :], tm=256)

    x1, h2 = _mix_call(sink_logit[l], q, k, v, kc, vc, hf, hb, rg, ga, gb, x, mod, w_o_rnn[l].astype(BF16),
                       w_o_attn[l].astype(BF16), w_out[l].astype(BF16), norm2_w[l][None], nqb=4)
    return _ffn_call(h2, x1, mod, w_up[l].astype(BF16), ffn_conv_w[l], ffn_conv_b[l][None],
                     w_down[l].astype(BF16), final_norm_w[None], tm=512)
```

```python
import functools

import numpy as np
import jax
import jax.numpy as jnp
from jax import lax
from jax.experimental import pallas as pl
from jax.experimental.pallas import tpu as pltpu

F32 = jnp.float32
BF16 = jnp.bfloat16

D_MODEL = 1024
N_HEADS = 8
N_KV_HEADS = 2
HEAD_DIM = 128
Q_PER_KV = N_HEADS // N_KV_HEADS
ATTN_WIDTH = N_HEADS * HEAD_DIM
KV_WIDTH = N_KV_HEADS * HEAD_DIM
WINDOW = 128
BLOCK = 128
GRID_W = 64
ROPE_THETA = 10000.0
D_RNN = D_MODEL
N_LRU_BLOCKS = 16
LRU_BLOCK = D_RNN // N_LRU_BLOCKS
LRU_C = 8.0
LRU_CONV_W = 4
LRU_CONV_LEFT = 2
N_DIRS = 2
D_FF = 3 * D_MODEL
FFN_CONV_W = 3
N_MOD = 6
EPS = 1e-6
NEG_INF = -1e30

COL_R = 0
COL_RG = COL_R + D_RNN
COL_Q = COL_RG + D_RNN
COL_K = COL_Q + ATTN_WIDTH
COL_V = COL_K + KV_WIDTH
COL_GA = COL_V + KV_WIDTH
COL_GB = COL_GA + D_MODEL
D_IN = COL_GB + D_MODEL

SUBLANES = 8
BF16_SUBLANES = 16
LANES = 128
MXU_DIM = 256
VMEM_LIMIT = 56 * 1024 * 1024

LANE_GROUPS = D_RNN // LANES
GATE_TILE = MXU_DIM
GATE_TILES = D_RNN // GATE_TILE
FFN_CHUNK = 512
FFN_DOWNS_PER_CHUNK = (0, 0, 1, 1, 2, 8)
MIX_FILLERS_PER_UNIT = (2, 2, 1, 2, 1, 2, 1, 2)
HALO = BF16_SUBLANES
LOG2E = float(np.log2(np.e))
assert BLOCK == LANES and WINDOW >= BLOCK - 1
TINY = 1e-30


GELU_K1 = float(np.sqrt(2.0 / np.pi))
GELU_K2 = GELU_K1 * 0.044715


def _gelu(v):
    return (0.5 * v) * (1.0 + jnp.tanh(v * (GELU_K1 + GELU_K2 * (v * v))))


def _sigmoid(v):
    return 0.5 * jnp.tanh(0.5 * v) + 0.5


def _const_spec(shape):
    nd = len(shape)
    return pl.BlockSpec(shape, lambda *_: (0,) * nd, pipeline_mode=pl.Buffered(1))


def _mod_kernel(c_ref, w_ref, b_ref, o_ref):
    c = c_ref[...]
    s = c * _sigmoid(c)
    o_ref[...] = jnp.dot(s, w_ref[...], preferred_element_type=F32) + b_ref[...]


def _mod_call(cs, w_mod, b_mod):
    n_out = w_mod.shape[1]
    tn = n_out // 4
    return pl.pallas_call(
        _mod_kernel,
        grid=(n_out // tn,),
        in_specs=[pl.BlockSpec((SUBLANES, D_MODEL), lambda j: (0, 0)),
                  pl.BlockSpec((D_MODEL, tn), lambda j: (0, j)),
                  pl.BlockSpec((1, tn), lambda j: (0, j))],
        out_specs=pl.BlockSpec((SUBLANES, tn), lambda j: (0, j)),
        out_shape=jax.ShapeDtypeStruct((SUBLANES, n_out), F32),
        compiler_params=pltpu.CompilerParams(dimension_semantics=("arbitrary",),
                                             vmem_limit_bytes=VMEM_LIMIT),
        name="mod",
    )(cs, w_mod, b_mod)


def _inproj_kernel(*refs, plan, rope, tm, nb, n_steps):
    x_ref, xp_ref, xn_ref, mod_ref, nw_ref, w_ref, cw_ref, cb_ref = refs[:8]
    n_in = 10 if rope else 8
    out_refs = refs[n_in:n_in + len(plan)]
    h_scr, r_scr = refs[-2:]
    j = pl.program_id(0)
    i = jnp.minimum(j, n_steps - 2) % nb
    slot = j % 2
    h_in = h_scr.at[1 - slot]

    def norm_mod(x):
        ms = jnp.mean(x * x, axis=-1, keepdims=True)
        y = x * lax.rsqrt(ms + EPS) * nw_ref[...]
        return y * (1.0 + mod_ref[0, 1:2, :]) + mod_ref[0, 0:1, :]

    def norm_stage():
        h_scr[slot, pl.ds(0, HALO), :] = jnp.where(i > 0, norm_mod(xp_ref[0]), 0.0).astype(BF16)
        h_scr[slot, pl.ds(HALO, tm), :] = norm_mod(x_ref[0]).astype(BF16)
        h_scr[slot, pl.ds(HALO + tm, HALO), :] = jnp.where(i < nb - 1, norm_mod(xn_ref[0]), 0.0).astype(BF16)

    def piece(c0, kind, o_ref, p):
        cs = slice(p * MXU_DIM, (p + 1) * MXU_DIM)
        wc = slice(c0 + p * MXU_DIM, c0 + (p + 1) * MXU_DIM)
        if kind == "conv":
            r = jnp.dot(h_in[...], w_ref[:, wc], preferred_element_type=F32)
            for q in range(MXU_DIM // LANES):
                lg = p * (MXU_DIM // LANES) + q
                ls = slice(lg * LANES, (lg + 1) * LANES)
                r_scr[lg, pl.ds(0, tm + 2 * HALO, stride=2), :] = r[:, q * LANES:(q + 1) * LANES]
                taps = [r_scr[lg, pl.ds(2 * (HALO - LRU_CONV_LEFT + k), tm, stride=2), :] for k in range(LRU_CONV_W)]
                o_ref[0, :, ls] = 0.5 * cb_ref[:, ls] + sum((0.5 * cw_ref[k:k + 1, ls]) * taps[k]
                                                            for k in range(LRU_CONV_W))
            return
        res = jnp.dot(h_in[pl.ds(HALO, tm), :], w_ref[:, wc], preferred_element_type=F32)
        if kind in ("rope_q", "rope_k"):
            scale = HEAD_DIM ** -0.5 * LOG2E if kind == "rope_q" else None
            cos, sin = refs[8][...], refs[9][...]
            lane = lax.broadcasted_iota(jnp.int32, (tm, HEAD_DIM), 1)
            low = (lane % (HEAD_DIM // 2)) < (HEAD_DIM // 4)
            for hh in range(MXU_DIM // HEAD_DIM):
                v = res[:, hh * HEAD_DIM:(hh + 1) * HEAD_DIM]
                partner = jnp.where(low, pltpu.roll(v, HEAD_DIM - HEAD_DIM // 4, 1), pltpu.roll(v, HEAD_DIM // 4, 1))
                rot = v * cos + partner * sin
                if scale is not None:
                    rot = rot * scale
                o_ref[0, :, p * MXU_DIM + hh * HEAD_DIM:p * MXU_DIM + (hh + 1) * HEAD_DIM] = rot.astype(o_ref.dtype)
        else:
            act = {"gelu": _gelu, "sigmoid": _sigmoid, "bf16": lambda t: t}[kind]
            o_ref[0, :, cs] = act(res).astype(o_ref.dtype)

    def project():
        by_kind = {}
        for (c0, width, kind), o_ref in zip(plan, out_refs):
            by_kind.setdefault(kind, []).extend(
                functools.partial(piece, c0, kind, o_ref, p) for p in range(width // MXU_DIM))
        take = lambda kind: by_kind[kind].pop(0)() if by_kind.get(kind) else None
        take("conv"), take("sigmoid")
        norm_stage()
        take("sigmoid"), take("bf16")
        while by_kind.get("conv"):
            take("conv"), take("sigmoid"), take("sigmoid"), take("bf16")
        while by_kind.get("rope_q") or by_kind.get("rope_k"):
            take("rope_q"), take("gelu"), take("rope_k"), take("bf16")
        for kind in list(by_kind):
            while by_kind[kind]:
                take(kind)

    pl.when(j == 0)(norm_stage)
    pl.when(j > 0)(project)


def _inproj_call(x, mod, mod_row0, nw, w_in, conv_w, conv_b, tables, plan, tm):
    bsz, length, _ = x.shape
    rope = tables is not None
    nb = length // tm
    n_steps = bsz * nb + 1
    hb = tm // HALO
    n_hb = length // HALO
    nrm = lambda j: divmod(jnp.minimum(j, n_steps - 2), nb)
    prj = lambda j: divmod(jnp.maximum(j - 1, 0), nb)
    tok_out = lambda j: (prj(j)[0], prj(j)[1], 0)
    in_specs = [pl.BlockSpec((1, tm, D_MODEL), lambda j: (nrm(j)[0], nrm(j)[1], 0)),
                pl.BlockSpec((1, HALO, D_MODEL), lambda j: (nrm(j)[0], jnp.maximum(nrm(j)[1] * hb - 1, 0), 0)),
                pl.BlockSpec((1, HALO, D_MODEL),
                             lambda j: (nrm(j)[0], jnp.minimum((nrm(j)[1] + 1) * hb, n_hb - 1), 0)),
                pl.BlockSpec((1, N_MOD, D_MODEL), lambda j: (nrm(j)[0] * (mod_row0 == 0) + mod_row0, 0, 0)),
                _const_spec((1, D_MODEL)),
                _const_spec((D_MODEL, D_IN)),
                _const_spec((LRU_CONV_W, D_RNN)),
                _const_spec((1, D_RNN))]
    args = [x, x, x, mod, nw, w_in, conv_w, conv_b]
    if rope:
        in_specs += [pl.BlockSpec((tm, HEAD_DIM), lambda j: (prj(j)[1], 0))] * 2
        args += list(tables)
    out_specs, out_shape = [], []
    for (_, width, kind) in plan:
        dt = F32 if kind == "conv" else BF16
        out_specs.append(pl.BlockSpec((1, tm, width), tok_out))
        out_shape.append(jax.ShapeDtypeStruct((bsz, length, width), dt))
    return pl.pallas_call(
        functools.partial(_inproj_kernel, plan=tuple(plan), rope=rope, tm=tm, nb=nb, n_steps=n_steps),
        grid=(n_steps,),
        in_specs=in_specs, out_specs=out_specs, out_shape=out_shape,
        scratch_shapes=[pltpu.VMEM((2, tm + 2 * HALO, D_MODEL), BF16),
                        pltpu.VMEM((D_RNN // LANES, 2 * (tm + 2 * HALO), LANES), F32)],
        compiler_params=pltpu.CompilerParams(dimension_semantics=("arbitrary",),
                                             vmem_limit_bytes=VMEM_LIMIT),
        name="inproj_x" if rope else "inproj_ctx",
    )(*args)


def _scan_kernel(xf_ref, xb_ref, xc_ref, wg_ref, bg_ref, lam_ref, hf_ref, hb_ref,
                 a3, b3, h3, hc_scr, *, tm, nb, clen, bsz):
    i = pl.program_id(0)
    chains = [(d, b) for d in range(N_DIRS) for b in range(bsz)]
    lg_per_tile = GATE_TILE // LANES

    def gate_piece(src_ref, ws, d, b, ch, kt, n):
        lam = lam_ref[d]
        softplus = jnp.maximum(-lam, 0.0) + jnp.log(1.0 + jnp.exp(-jnp.abs(lam)))
        hd2 = (-0.5 * LRU_C * LOG2E) * softplus
        cs = slice(kt * GATE_TILE, (kt + 1) * GATE_TILE)
        xk = src_ref[b, pl.ds(0, n), cs]
        g = jnp.dot(xk.astype(BF16), wg_ref[d, kt], preferred_element_type=F32)
        t_r = jnp.tanh(g[:, :GATE_TILE] + 0.5 * bg_ref[d, 0:1, cs])
        t_i = jnp.tanh(g[:, GATE_TILE:] + 0.5 * bg_ref[d, 1:2, cs])
        a = jnp.exp2(hd2[:, cs] * t_r + hd2[:, cs])
        y = 1.0 - a * a
        bb = (y * lax.rsqrt(jnp.maximum(y, TINY)) * xk) * (t_i + 1.0)
        for gg in range(lg_per_tile):
            lg = kt * lg_per_tile + gg
            ls = slice(gg * LANES, (gg + 1) * LANES)
            a3[ws, ch, pl.ds(0, n // SUBLANES), pl.ds(lg, SUBLANES, stride=LANE_GROUPS), :] = (
                a[:, ls].reshape(n // SUBLANES, SUBLANES, LANES))
            b3[ws, ch, pl.ds(0, n // SUBLANES), pl.ds(lg, SUBLANES, stride=LANE_GROUPS), :] = (
                bb[:, ls].reshape(n // SUBLANES, SUBLANES, LANES))

    def scan_group(rs, n8, tau, hs, keep):
        for rho in range(SUBLANES):
            for ch, (d, b) in enumerate(chains):
                tt, rr = (tau, rho) if d == 0 else (n8 - 1 - tau, SUBLANES - 1 - rho)
                win = pl.ds(rr * LANE_GROUPS, LANE_GROUPS)
                h = a3[rs, ch, tt, win, :] * hs[ch] + b3[rs, ch, tt, win, :]
                hs[ch] = h
                if keep:
                    h3[ch, tt, win, :] = h

    def emit(rows):
        r0, r1 = rows
        for ch, (d, b) in enumerate(chains):
            lo, hi = (r0, r1) if d == 0 else (tm - r1, tm - r0)
            o_ref = hf_ref if d == 0 else hb_ref
            for lg in range(LANE_GROUPS):
                slab = h3[ch, pl.ds(lo // SUBLANES, (hi - lo) // SUBLANES), pl.ds(lg, SUBLANES, stride=LANE_GROUPS), :]
                o_ref[b, pl.ds(lo, hi - lo), lg * LANES:(lg + 1) * LANES] = (
                    slab.reshape(hi - lo, LANES).astype(o_ref.dtype))

    @pl.when(i == 0)
    def _():
        for ch, (d, b) in enumerate(chains):
            for kt in range(GATE_TILES):
                gate_piece(xc_ref, 1, d, b, ch, kt, clen)

        def body(tau, hs):
            hs = list(hs)
            scan_group(1, clen // SUBLANES, tau, hs, False)
            return tuple(hs)

        hs = lax.fori_loop(0, clen // SUBLANES, body, (jnp.zeros((LANE_GROUPS, LANES), F32),) * len(chains))
        for ch in range(len(chains)):
            hc_scr[ch] = hs[ch]

    def step(ws, rs, gate, scan):
        n8 = tm // SUBLANES
        pieces = [(d, b, ch, kt) for ch, (d, b) in enumerate(chains) for kt in range(GATE_TILES)]
        per_piece = n8 // len(pieces)
        hs = [hc_scr[ch] for ch in range(len(chains))] if scan else None
        tau = 0
        for (d, b, ch, kt) in pieces:
            if gate:
                gate_piece(xf_ref if d == 0 else xb_ref, ws, d, b, ch, kt, tm)
            for _ in range(per_piece if scan else 0):
                scan_group(rs, n8, tau, hs, True)
                tau += 1
                if tau == n8 // 2:
                    emit((0, tm // 2))
        if scan:
            emit((tm // 2, tm))
            for ch in range(len(chains)):
                hc_scr[ch] = hs[ch]

    pl.when(i == 0)(functools.partial(step, 0, 1, True, False))
    for parity in range(2):
        pl.when((i > 0) & (i < nb) & (i % 2 == parity))(functools.partial(step, parity, 1 - parity, True, True))
    pl.when(i == nb)(functools.partial(step, nb % 2, 1 - nb % 2, False, True))


def _scan_call(xc, xc_ctx, wg, bg, lam, tm):
    bsz, length, ch = xc.shape
    clen = xc_ctx.shape[1]
    nb = length // tm
    n_chains = N_DIRS * bsz
    assert clen <= tm and (tm // SUBLANES) % (n_chains * GATE_TILES) == 0
    rows8 = tm // SUBLANES
    slab = (n_chains, rows8, LANE_GROUPS * SUBLANES, LANES)
    gi = lambda i: jnp.minimum(i, nb - 1)
    si = lambda i: jnp.maximum(i - 1, 0)
    return pl.pallas_call(
        functools.partial(_scan_kernel, tm=tm, nb=nb, clen=clen, bsz=bsz),
        grid=(nb + 1,),
        in_specs=[pl.BlockSpec((bsz, tm, ch), lambda i: (0, gi(i), 0)),
                  pl.BlockSpec((bsz, tm, ch), lambda i: (0, nb - 1 - gi(i), 0)),
                  _const_spec((bsz, clen, ch)),
                  _const_spec((N_DIRS, GATE_TILES, GATE_TILE, 2 * GATE_TILE)),
                  _const_spec((N_DIRS, 2, ch)),
                  _const_spec((N_DIRS, 1, ch))],
        out_specs=[pl.BlockSpec((bsz, tm, ch), lambda i: (0, si(i), 0)),
                   pl.BlockSpec((bsz, tm, ch), lambda i: (0, nb - 1 - si(i), 0))],
        out_shape=[jax.ShapeDtypeStruct((bsz, length, ch), BF16)] * 2,
        scratch_shapes=[pltpu.VMEM((2,) + slab, F32), pltpu.VMEM((2,) + slab, F32), pltpu.VMEM(slab, F32),
                        pltpu.VMEM((n_chains, LANE_GROUPS, LANES), F32)],
        compiler_params=pltpu.CompilerParams(dimension_semantics=("arbitrary",),
                                             vmem_limit_bytes=VMEM_LIMIT),
        name="scan",
    )(xc, xc, xc_ctx, wg, bg, lam)


def _attention(i, sink_ref, q_ref, k_ref, kp_ref, kn_ref, v_ref, vp_ref, vn_ref, kc_ref, vc_ref,
               ya_scr, s_scr, kcat, vcat, vctx, *, nqb, length):
    tq = nqb * BLOCK
    clen = kc_ref.shape[1]
    kcat[pl.ds(0, BLOCK), :] = kp_ref[0]
    kcat[pl.ds(BLOCK, tq), :] = k_ref[0]
    kcat[pl.ds(BLOCK + tq, BLOCK), :] = kn_ref[0]
    for g in range(N_KV_HEADS):
        gs = slice(g * HEAD_DIM, (g + 1) * HEAD_DIM)
        vs = slice(2 * g * HEAD_DIM, (2 * g + 1) * HEAD_DIM)
        os_ = slice((2 * g + 1) * HEAD_DIM, (2 * g + 2) * HEAD_DIM)
        vcat[pl.ds(0, BLOCK), vs] = vp_ref[0, :, gs]
        vcat[pl.ds(BLOCK, tq), vs] = v_ref[0, :, gs]
        vcat[pl.ds(BLOCK + tq, BLOCK), vs] = vn_ref[0, :, gs]
        vcat[:, os_] = jnp.ones((tq + 2 * BLOCK, HEAD_DIM), BF16)
        vctx[:, vs] = vc_ref[0, :, gs]
        vctx[:, os_] = jnp.ones((clen, HEAD_DIM), BF16)

    qi = lax.broadcasted_iota(jnp.int32, (BLOCK, 3 * BLOCK), 0)
    kj = lax.broadcasted_iota(jnp.int32, (BLOCK, 3 * BLOCK), 1)
    rel = kj - qi
    band = (rel >= BLOCK - WINDOW) & (rel <= BLOCK + WINDOW)
    nt = (((1,), (1,)), ((), ()))
    rows = Q_PER_KV * BLOCK
    masks = []
    for jq in range(nqb):
        kpos = (i * nqb + jq - 1) * BLOCK + kj
        masks.append((band & (kpos >= 0) & (kpos < length))[None])

    n_loc = 3 * BLOCK

    def scores(jq, g, buf):
        gs = slice(g * HEAD_DIM, (g + 1) * HEAD_DIM)
        q4 = jnp.concatenate(
            [q_ref[0, pl.ds(jq * BLOCK, BLOCK), h * HEAD_DIM:(h + 1) * HEAD_DIM]
             for h in range(g * Q_PER_KV, (g + 1) * Q_PER_KV)], axis=0)
        s_scr[buf, :, 0:n_loc] = lax.dot_general(q4, kcat[pl.ds(jq * BLOCK, n_loc), gs], nt,
                                                 preferred_element_type=F32)
        s_scr[buf, :, n_loc:n_loc + clen] = lax.dot_general(q4, kc_ref[0, :, gs], nt, preferred_element_type=F32)

    def softmax_values(jq, g, buf):
        vos = slice(2 * g * HEAD_DIM, (2 * g + 2) * HEAD_DIM)
        heads = [g * Q_PER_KV + hh for hh in range(Q_PER_KV)]
        sink = jnp.concatenate([jnp.full((BLOCK, 1), sink_ref[h] * LOG2E, F32) for h in heads], axis=0)
        cols = []
        for c in range(n_loc // BLOCK):
            s = s_scr[buf, :, c * BLOCK:(c + 1) * BLOCK]
            if c != 1:
                s = jnp.where(masks[jq][:, :, c * BLOCK:(c + 1) * BLOCK], s.reshape(Q_PER_KV, BLOCK, BLOCK),
                              NEG_INF).reshape(rows, BLOCK)
            cols.append(s)
        cols += [s_scr[buf, :, n_loc + c * LANES:n_loc + (c + 1) * LANES] for c in range(clen // LANES)]
        m = jnp.maximum(jnp.max(functools.reduce(jnp.maximum, cols), axis=-1, keepdims=True), sink)
        p = [jnp.exp2(s - m).astype(BF16) for s in cols]
        p_loc = jnp.concatenate(p[:n_loc // BLOCK], axis=1)
        p_ctx = jnp.concatenate(p[n_loc // BLOCK:], axis=1)
        o2 = (jnp.dot(p_loc, vcat[pl.ds(jq * BLOCK, n_loc), vos], preferred_element_type=F32)
              + jnp.dot(p_ctx, vctx[:, vos], preferred_element_type=F32))
        denom = o2[:, HEAD_DIM:] + jnp.exp2(sink - m)
        o = o2[:, :HEAD_DIM] * (1.0 / denom)
        for hh, h in enumerate(heads):
            ya_scr[pl.ds(jq * BLOCK, BLOCK), h * HEAD_DIM:(h + 1) * HEAD_DIM] = (
                o[hh * BLOCK:(hh + 1) * BLOCK].astype(ya_scr.dtype))

    order = [(jq, g) for jq in range(nqb) for g in range(N_KV_HEADS)]
    return [(functools.partial(scores, jq, g, u % 2), functools.partial(softmax_values, jq, g, u % 2))
            for u, (jq, g) in enumerate(order)]


def _mix_kernel(*refs, nqb, length, n_steps):
    (sink_ref, q_ref, k_ref, kp_ref, kn_ref, v_ref, vp_ref, vn_ref, kc_ref, vc_ref,
     hf_ref, hb_ref, rg_ref, ga_ref, gb_ref, x_ref, mod_ref, wr_ref, wa_ref, wo_ref, nw_ref,
     x1_ref, h2_ref, ya_scr, yr_scr, s_scr, kcat, vcat, vctx) = refs
    j = pl.program_id(0)
    slot = j % 2
    nb = length // (nqb * BLOCK)

    @pl.when(j == 0)
    def _():
        ya_scr[1] = jnp.zeros(ya_scr.shape[1:], ya_scr.dtype)

    units = _attention(jnp.minimum(j, n_steps - 2) % nb, sink_ref, q_ref, k_ref, kp_ref, kn_ref, v_ref, vp_ref,
                       vn_ref, kc_ref, vc_ref, ya_scr.at[slot], s_scr, kcat, vcat, vctx, nqb=nqb, length=length)

    state = {}

    def rnn_in():
        y_rnn = (hf_ref[0].astype(F32) + hb_ref[0].astype(F32)) * rg_ref[0].astype(F32)
        yr_scr[...] = y_rnn.astype(BF16)

    def branch_piece(p, w_ref, src, key):
        cs = slice(p * MXU_DIM, (p + 1) * MXU_DIM)
        state[key, p] = jnp.dot(src(), w_ref[:, cs], preferred_element_type=F32)

    def out_piece(p):
        cs = slice(p * MXU_DIM, (p + 1) * MXU_DIM)
        y = (ga_ref[0, :, cs].astype(F32) * state.pop(("r", p))
             + gb_ref[0, :, cs].astype(F32) * state.pop(("a", p)))
        part = jnp.dot(y.astype(BF16), wo_ref[cs, :], preferred_element_type=F32)
        state["z"] = part if "z" not in state else state["z"] + part

    n_p = D_MODEL // MXU_DIM
    att = [functools.partial(branch_piece, p, wa_ref, lambda: ya_scr[1 - slot], "a") for p in range(n_p)]
    rnn = [functools.partial(branch_piece, p, wr_ref, lambda: yr_scr[...], "r") for p in range(n_p)]
    outs = [functools.partial(out_piece, p) for p in range(n_p)]
    fillers = [rnn_in] + att + rnn[:n_p - 1] + [outs[0], rnn[n_p - 1]] + outs[1:]
    per_unit = MIX_FILLERS_PER_UNIT
    assert sum(per_unit) == len(fillers) and len(per_unit) == len(units)
    units[0][0]()
    for u, (_, softmax_values) in enumerate(units):
        if u + 1 < len(units):
            units[u + 1][0]()
        for f in fillers[:per_unit[u]]:
            f()
        fillers = fillers[per_unit[u]:]
        softmax_values()

    x1 = x_ref[0] + mod_ref[0, 2:3, :] * state["z"]
    x1_ref[0] = x1
    ms = jnp.mean(x1 * x1, axis=-1, keepdims=True)
    n2 = x1 * lax.rsqrt(ms + EPS) * nw_ref[...]
    h2_ref[0] = (n2 * (1.0 + mod_ref[0, 4:5, :]) + mod_ref[0, 3:4, :]).astype(h2_ref.dtype)


def _mix_call(sink, q, k, v, kc, vc, hf, hb, rg, ga, gb, x, mod, w_o_rnn, w_o_attn, w_out, nw2, nqb):
    bsz, length, _ = x.shape
    clen = kc.shape[1]
    tm = nqb * BLOCK
    n_blk = length // BLOCK
    nb = length // tm
    n_steps = bsz * nb + 1
    att = lambda j: divmod(jnp.minimum(j, n_steps - 2), nb)
    mrg = lambda j: divmod(jnp.maximum(j - 1, 0), nb)
    main = lambda j: (att(j)[0], att(j)[1], 0)
    prev = lambda j: (att(j)[0], jnp.maximum(att(j)[1] * nqb - 1, 0), 0)
    nxt = lambda j: (att(j)[0], jnp.minimum((att(j)[1] + 1) * nqb, n_blk - 1), 0)
    kv_specs = [pl.BlockSpec((1, tm, KV_WIDTH), main),
                pl.BlockSpec((1, BLOCK, KV_WIDTH), prev),
                pl.BlockSpec((1, BLOCK, KV_WIDTH), nxt)]
    ctx_spec = pl.BlockSpec((1, clen, KV_WIDTH), lambda j: (att(j)[0], 0, 0))
    blk = pl.BlockSpec((1, tm, D_MODEL), lambda j: (mrg(j)[0], mrg(j)[1], 0))
    return pl.pallas_call(
        functools.partial(_mix_kernel, nqb=nqb, length=length, n_steps=n_steps),
        grid=(n_steps,),
        in_specs=[pl.BlockSpec(memory_space=pltpu.SMEM),
                  pl.BlockSpec((1, tm, ATTN_WIDTH), main)] + kv_specs + kv_specs + [ctx_spec, ctx_spec]
                 + [blk, blk, blk, blk, blk, blk,
                    pl.BlockSpec((1, N_MOD, D_MODEL), lambda j: (mrg(j)[0], 0, 0)),
                    _const_spec((D_RNN, D_MODEL)), _const_spec((ATTN_WIDTH, D_MODEL)),
                    _const_spec((D_MODEL, D_MODEL)), _const_spec((1, D_MODEL))],
        out_specs=[blk, blk],
        out_shape=[jax.ShapeDtypeStruct((bsz, length, D_MODEL), F32),
                   jax.ShapeDtypeStruct((bsz, length, D_MODEL), BF16)],
        scratch_shapes=[pltpu.VMEM((2, tm, ATTN_WIDTH), BF16),
                        pltpu.VMEM((tm, D_RNN), BF16),
                        pltpu.VMEM((2, Q_PER_KV * BLOCK, 3 * BLOCK + clen), F32),
                        pltpu.VMEM((tm + 2 * BLOCK, KV_WIDTH), BF16),
                        pltpu.VMEM((tm + 2 * BLOCK, 2 * KV_WIDTH), BF16),
                        pltpu.VMEM((clen, 2 * KV_WIDTH), BF16)],
        compiler_params=pltpu.CompilerParams(dimension_semantics=("arbitrary",),
                                             vmem_limit_bytes=VMEM_LIMIT),
        name="mix",
    )(sink, q, k, k, k, v, v, v, kc, vc, hf, hb, rg, ga, gb, x, mod, w_o_rnn, w_o_attn, w_out, nw2)


def _ffn_kernel(h_ref, hp_ref, hn_ref, x1_ref, mod_ref, wu_ref, cw_ref, cb_ref, wd_ref, fw_ref,
                o_ref, hbuf, ua_scr, uv_scr, g_scr, *, tm, nb, n_steps):
    j = pl.program_id(0)
    i = jnp.minimum(j, n_steps - 2) % nb
    ring = j % 2

    def stage_rows():
        hbuf[pl.ds(0, HALO), :] = jnp.where(i > 0, hp_ref[0], jnp.zeros_like(hp_ref[0]))
        hbuf[pl.ds(HALO, tm), :] = h_ref[0]
        hbuf[pl.ds(HALO + tm, HALO), :] = jnp.where(i < nb - 1, hn_ref[0], jnp.zeros_like(hn_ref[0]))

    rows = tm + 2 * HALO

    n_chunks = D_FF // FFN_CHUNK
    pieces = FFN_CHUNK // MXU_DIM
    lg_per_piece = MXU_DIM // LANES

    def up_piece(u_scr, c, p, c0):
        u = jnp.dot(hbuf[...], wu_ref[:, c0 + p * MXU_DIM:c0 + (p + 1) * MXU_DIM], preferred_element_type=F32)
        for q in range(lg_per_piece):
            u_scr[c % 2, p * lg_per_piece + q, pl.ds(0, rows, stride=2), :] = u[:, q * LANES:(q + 1) * LANES]

    def conv(u_scr, slot, lg, c0):
        cs = slice(c0 + lg * LANES, c0 + (lg + 1) * LANES)
        taps = [u_scr[slot, lg, pl.ds(2 * (HALO - FFN_CONV_W // 2 + k), tm, stride=2), :] for k in range(FFN_CONV_W)]
        return cb_ref[:, cs] + sum(cw_ref[k:k + 1, cs] * taps[k] for k in range(FFN_CONV_W))

    def act_piece(c, lg):
        ca, cv = c * FFN_CHUNK, D_FF + c * FFN_CHUNK
        g = _gelu(conv(ua_scr, c % 2, lg, ca)) * conv(uv_scr, c % 2, lg, cv)
        g_scr[ring, :, ca + lg * LANES:ca + (lg + 1) * LANES] = g.astype(BF16)

    half = D_MODEL // 2

    z = [None, None]

    def down_piece(c, hh):
        ks = slice(c * FFN_CHUNK, (c + 1) * FFN_CHUNK)
        part = jnp.dot(g_scr[1 - ring, :, ks], wd_ref[ks, hh * half:(hh + 1) * half], preferred_element_type=F32)
        z[hh] = part if z[hh] is None else z[hh] + part

    @pl.when(j == 0)
    def _():
        g_scr[1] = jnp.zeros(g_scr.shape[1:], g_scr.dtype)

    downs = [functools.partial(down_piece, c, hh) for c in range(n_chunks) for hh in range(2)]
    n_down = list(FFN_DOWNS_PER_CHUNK[:-1]) + [len(downs)]
    stage_rows()
    for p in range(pieces):
        up_piece(ua_scr, 0, p, 0)
        up_piece(uv_scr, 0, p, D_FF)
    for c in range(n_chunks):
        mm = []
        if c + 1 < n_chunks:
            for p in range(pieces):
                mm.append(functools.partial(up_piece, ua_scr, c + 1, p, (c + 1) * FFN_CHUNK))
                mm.append(functools.partial(up_piece, uv_scr, c + 1, p, D_FF + (c + 1) * FFN_CHUNK))
        mm += downs[:n_down[c]]
        downs = downs[n_down[c]:]
        acts = [functools.partial(act_piece, c, lg) for lg in range(FFN_CHUNK // LANES)]
        while mm or acts:
            if mm:
                mm.pop(0)()
            if acts:
                acts.pop(0)()
    x2 = x1_ref[0] + mod_ref[0, 5:6, :] * jnp.concatenate(z, axis=1)
    ms = jnp.mean(x2 * x2, axis=-1, keepdims=True)
    o_ref[0] = x2 * lax.rsqrt(ms + EPS) * fw_ref[...]


def _ffn_call(h2, x1, mod, w_up, conv_w, conv_b, w_down, fw, tm):
    bsz, length, _ = x1.shape
    nb = length // tm
    n_steps = bsz * nb + 1
    hb = tm // HALO
    n_hb = length // HALO
    act = lambda j: divmod(jnp.minimum(j, n_steps - 2), nb)
    dwn = lambda j: divmod(jnp.maximum(j - 1, 0), nb)
    tok_out = lambda j: (dwn(j)[0], dwn(j)[1], 0)
    return pl.pallas_call(
        functools.partial(_ffn_kernel, tm=tm, nb=nb, n_steps=n_steps),
        grid=(n_steps,),
        in_specs=[pl.BlockSpec((1, tm, D_MODEL), lambda j: (act(j)[0], act(j)[1], 0)),
                  pl.BlockSpec((1, HALO, D_MODEL), lambda j: (act(j)[0], jnp.maximum(act(j)[1] * hb - 1, 0), 0)),
                  pl.BlockSpec((1, HALO, D_MODEL),
                               lambda j: (act(j)[0], jnp.minimum((act(j)[1] + 1) * hb, n_hb - 1), 0)),
                  pl.BlockSpec((1, tm, D_MODEL), tok_out),
                  pl.BlockSpec((1, N_MOD, D_MODEL), lambda j: (dwn(j)[0], 0, 0)),
                  _const_spec((D_MODEL, 2 * D_FF)), _const_spec((FFN_CONV_W, 2 * D_FF)),
                  _const_spec((1, 2 * D_FF)), _const_spec((D_FF, D_MODEL)), _const_spec((1, D_MODEL))],
        out_specs=pl.BlockSpec((1, tm, D_MODEL), tok_out),
        out_shape=jax.ShapeDtypeStruct((bsz, length, D_MODEL), F32),
        scratch_shapes=[pltpu.VMEM((tm + 2 * HALO, D_MODEL), BF16),
                        pltpu.VMEM((2, FFN_CHUNK // LANES, 2 * (tm + 2 * HALO), LANES), F32),
                        pltpu.VMEM((2, FFN_CHUNK // LANES, 2 * (tm + 2 * HALO), LANES), F32),
                        pltpu.VMEM((2, tm, D_FF), BF16)],
        compiler_params=pltpu.CompilerParams(dimension_semantics=("arbitrary",),
                                             vmem_limit_bytes=VMEM_LIMIT),
        name="ffn",
    )(h2, h2, h2, x1, mod, w_up, conv_w, conv_b, w_down, fw)


def _rope_tables(length):
    pos = np.arange(length)
    row, col = pos // GRID_W, pos % GRID_W
    half = HEAD_DIM // 2
    inv = ROPE_THETA ** (-np.arange(0, half, 2, dtype=np.float64) / half)
    ar, ac = row[:, None] * inv, col[:, None] * inv
    cos = np.concatenate([np.cos(ar), np.cos(ar), np.cos(ac), np.cos(ac)], axis=1)
    sin = np.concatenate([-np.sin(ar), np.sin(ar), -np.sin(ac), np.sin(ac)], axis=1)
    return jnp.asarray(cos, F32), jnp.asarray(sin, F32)


def _gate_weights(w_a, w_x):
    per = GATE_TILE // LRU_BLOCK
    eye = jnp.eye(per, dtype=F32)

    def bd(w):
        w = w.reshape(N_DIRS, GATE_TILES, per, LRU_BLOCK, LRU_BLOCK)
        return jnp.einsum('dtpcn,pq->dtpcqn', w, eye).reshape(N_DIRS, GATE_TILES, GATE_TILE, GATE_TILE)

    return jnp.concatenate([bd(w_a), bd(w_x)], axis=-1).astype(BF16)


def kernel(x, c, ctx, c_ctx, w_mod, b_mod, norm1_w, w_in, lru_conv_w, lru_conv_b, gate_a_w, gate_a_b,
           gate_x_w, gate_x_b, lru_lambda, sink_logit, w_o_rnn, w_o_attn, w_out, norm2_w, w_up,
           ffn_conv_w, ffn_conv_b, w_down, final_norm_w):
    bsz, length, _ = x.shape
    l = 0

    cs = jnp.concatenate([c, c_ctx[None], jnp.zeros((SUBLANES - bsz - 1, D_MODEL), F32)], axis=0)
    mod = _mod_call(cs, w_mod[l], b_mod[l][None]).reshape(SUBLANES, N_MOD, D_MODEL)

    w_in_b = w_in[l].astype(BF16)
    nw1 = norm1_w[l][None]
    cw, cb = lru_conv_w[l], lru_conv_b[l][None]

    plan_x = [(COL_R, D_RNN, "conv"), (COL_RG, D_RNN, "gelu"), (COL_Q, ATTN_WIDTH, "rope_q"),
              (COL_K, KV_WIDTH, "rope_k"), (COL_V, KV_WIDTH, "bf16"), (COL_GA, D_MODEL, "sigmoid"),
              (COL_GB, D_MODEL, "sigmoid")]
    xc, rg, q, k, v, ga, gb = _inproj_call(x, mod, 0, nw1, w_in_b, cw, cb, _rope_tables(length), plan_x, tm=512)
    plan_c = [(COL_R, D_RNN, "conv"), (COL_K, KV_WIDTH, "bf16"), (COL_V, KV_WIDTH, "bf16")]
    xc_ctx, kc, vc = _inproj_call(ctx, mod, bsz, nw1, w_in_b, cw, cb, None, plan_c, tm=ctx.shape[1])

    wg = _gate_weights(gate_a_w[l], gate_x_w[l])
    bg = jnp.stack([gate_a_b[l], gate_x_b[l]], axis=1)
    hf, hb = _scan_call(xc, xc_ctx, wg, bg, lru_lambda[l][:, None, :], tm=256)

    x1, h2 = _mix_call(sink_logit[l], q, k, v, kc, vc, hf, hb, rg, ga, gb, x, mod, w_o_rnn[l].astype(BF16),
                       w_o_attn[l].astype(BF16), w_out[l].astype(BF16), norm2_w[l][None], nqb=4)
    return _ffn_call(h2, x1, mod, w_up[l].astype(BF16), ffn_conv_w[l], ffn_conv_b[l][None],
                     w_down[l].astype(BF16), final_norm_w[None], tm=512)
```

```python
import functools

import numpy as np
import jax
import jax.numpy as jnp
from jax import lax
from jax.experimental import pallas as pl
from jax.experimental.pallas import tpu as pltpu

F32 = jnp.float32
BF16 = jnp.bfloat16

D_MODEL = 1024
N_HEADS = 8
N_KV_HEADS = 2
HEAD_DIM = 128
Q_PER_KV = N_HEADS // N_KV_HEADS
ATTN_WIDTH = N_HEADS * HEAD_DIM
KV_WIDTH = N_KV_HEADS * HEAD_DIM
WINDOW = 128
BLOCK = 128
GRID_W = 64
ROPE_THETA = 10000.0
D_RNN = D_MODEL
N_LRU_BLOCKS = 16
LRU_BLOCK = D_RNN // N_LRU_BLOCKS
LRU_C = 8.0
LRU_CONV_W = 4
LRU_CONV_LEFT = 2
N_DIRS = 2
D_FF = 3 * D_MODEL
FFN_CONV_W = 3
N_MOD = 6
EPS = 1e-6
NEG_INF = -1e30

COL_R = 0
COL_RG = COL_R + D_RNN
COL_Q = COL_RG + D_RNN
COL_K = COL_Q + ATTN_WIDTH
COL_V = COL_K + KV_WIDTH
COL_GA = COL_V + KV_WIDTH
COL_GB = COL_GA + D_MODEL
D_IN = COL_GB + D_MODEL

SUBLANES = 8
BF16_SUBLANES = 16
LANES = 128
MXU_DIM = 256
VMEM_LIMIT = 56 * 1024 * 1024

LANE_GROUPS = D_RNN // LANES
GATE_TILE = MXU_DIM
GATE_TILES = D_RNN // GATE_TILE
FFN_CHUNK = 512
FFN_DOWNS_PER_CHUNK = (0, 0, 1, 1, 2, 8)
MIX_FILLERS_PER_UNIT = (2, 2, 1, 2, 1, 2, 1, 2)
HALO = BF16_SUBLANES
LOG2E = float(np.log2(np.e))
assert BLOCK == LANES and WINDOW >= BLOCK - 1
TINY = 1e-30


GELU_K1 = float(np.sqrt(2.0 / np.pi))
GELU_K2 = GELU_K1 * 0.044715


def _gelu(v):
    return (0.5 * v) * (1.0 + jnp.tanh(v * (GELU_K1 + GELU_K2 * (v * v))))


def _sigmoid(v):
    return 0.5 * jnp.tanh(0.5 * v) + 0.5


def _const_spec(shape):
    nd = len(shape)
    return pl.BlockSpec(shape, lambda *_: (0,) * nd, pipeline_mode=pl.Buffered(1))


def _mod_kernel(c_ref, w_ref, b_ref, o_ref):
    c = c_ref[...]
    s = c * _sigmoid(c)
    o_ref[...] = jnp.dot(s, w_ref[...], preferred_element_type=F32) + b_ref[...]


def _mod_call(cs, w_mod, b_mod):
    n_out = w_mod.shape[1]
    tn = n_out // 4
    return pl.pallas_call(
        _mod_kernel,
        grid=(n_out // tn,),
        in_specs=[pl.BlockSpec((SUBLANES, D_MODEL), lambda j: (0, 0)),
                  pl.BlockSpec((D_MODEL, tn), lambda j: (0, j)),
                  pl.BlockSpec((1, tn), lambda j: (0, j))],
        out_specs=pl.BlockSpec((SUBLANES, tn), lambda j: (0, j)),
        out_shape=jax.ShapeDtypeStruct((SUBLANES, n_out), F32),
        compiler_params=pltpu.CompilerParams(dimension_semantics=("arbitrary",),
                                             vmem_limit_bytes=VMEM_LIMIT),
        name="mod",
    )(cs, w_mod, b_mod)


def _inproj_kernel(*refs, plan, rope, tm, nb, n_steps):
    x_ref, xp_ref, xn_ref, mod_ref, nw_ref, w_ref, cw_ref, cb_ref = refs[:8]
    n_in = 10 if rope else 8
    out_refs = refs[n_in:n_in + len(plan)]
    h_scr, r_scr = refs[-2:]
    j = pl.program_id(0)
    i = jnp.minimum(j, n_steps - 2) % nb
    slot = j % 2
    h_in = h_scr.at[1 - slot]

    def norm_mod(x):
        ms = jnp.mean(x * x, axis=-1, keepdims=True)
        y = x * lax.rsqrt(ms + EPS) * nw_ref[...]
        return y * (1.0 + mod_ref[0, 1:2, :]) + mod_ref[0, 0:1, :]

    def norm_stage():
        h_scr[slot, pl.ds(0, HALO), :] = jnp.where(i > 0, norm_mod(xp_ref[0]), 0.0).astype(BF16)
        h_scr[slot, pl.ds(HALO, tm), :] = norm_mod(x_ref[0]).astype(BF16)
        h_scr[slot, pl.ds(HALO + tm, HALO), :] = jnp.where(i < nb - 1, norm_mod(xn_ref[0]), 0.0).astype(BF16)

    def piece(c0, kind, o_ref, p):
        cs = slice(p * MXU_DIM, (p + 1) * MXU_DIM)
        wc = slice(c0 + p * MXU_DIM, c0 + (p + 1) * MXU_DIM)
        if kind == "conv":
            r = jnp.dot(h_in[...], w_ref[:, wc].astype(BF16), preferred_element_type=F32)
            for q in range(MXU_DIM // LANES):
                lg = p * (MXU_DIM // LANES) + q
                ls = slice(lg * LANES, (lg + 1) * LANES)
                r_scr[lg, pl.ds(0, tm + 2 * HALO, stride=2), :] = r[:, q * LANES:(q + 1) * LANES]
                taps = [r_scr[lg, pl.ds(2 * (HALO - LRU_CONV_LEFT + k), tm, stride=2), :] for k in range(LRU_CONV_W)]
                o_ref[0, :, ls] = 0.5 * cb_ref[:, ls] + sum((0.5 * cw_ref[k:k + 1, ls]) * taps[k]
                                                            for k in range(LRU_CONV_W))
            return
        res = jnp.dot(h_in[pl.ds(HALO, tm), :], w_ref[:, wc].astype(BF16), preferred_element_type=F32)
        if kind in ("rope_q", "rope_k"):
            scale = HEAD_DIM ** -0.5 * LOG2E if kind == "rope_q" else None
            cos, sin = refs[8][...], refs[9][...]
            lane = lax.broadcasted_iota(jnp.int32, (tm, HEAD_DIM), 1)
            low = (lane % (HEAD_DIM // 2)) < (HEAD_DIM // 4)
            for hh in range(MXU_DIM // HEAD_DIM):
                v = res[:, hh * HEAD_DIM:(hh + 1) * HEAD_DIM]
                partner = jnp.where(low, pltpu.roll(v, HEAD_DIM - HEAD_DIM // 4, 1), pltpu.roll(v, HEAD_DIM // 4, 1))
                rot = v * cos + partner * sin
                if scale is not None:
                    rot = rot * scale
                o_ref[0, :, p * MXU_DIM + hh * HEAD_DIM:p * MXU_DIM + (hh + 1) * HEAD_DIM] = rot.astype(o_ref.dtype)
        else:
            act = {"gelu": _gelu, "sigmoid": _sigmoid, "bf16": lambda t: t}[kind]
            o_ref[0, :, cs] = act(res).astype(o_ref.dtype)

    def project():
        by_kind = {}
        for (c0, width, kind), o_ref in zip(plan, out_refs):
            by_kind.setdefault(kind, []).extend(
                functools.partial(piece, c0, kind, o_ref, p) for p in range(width // MXU_DIM))
        take = lambda kind: by_kind[kind].pop(0)() if by_kind.get(kind) else None
        take("conv"), take("sigmoid")
        norm_stage()
        take("sigmoid"), take("bf16")
        while by_kind.get("conv"):
            take("conv"), take("sigmoid"), take("sigmoid"), take("bf16")
        while by_kind.get("rope_q") or by_kind.get("rope_k"):
            take("rope_q"), take("gelu"), take("rope_k"), take("bf16")
        for kind in list(by_kind):
            while by_kind[kind]:
                take(kind)

    pl.when(j == 0)(norm_stage)
    pl.when(j > 0)(project)


def _inproj_call(x, mod, mod_row0, nw, w_in, conv_w, conv_b, tables, plan, tm):
    bsz, length, _ = x.shape
    rope = tables is not None
    nb = length // tm
    n_steps = bsz * nb + 1
    hb = tm // HALO
    n_hb = length // HALO
    nrm = lambda j: divmod(jnp.minimum(j, n_steps - 2), nb)
    prj = lambda j: divmod(jnp.maximum(j - 1, 0), nb)
    tok_out = lambda j: (prj(j)[0], prj(j)[1], 0)
    in_specs = [pl.BlockSpec((1, tm, D_MODEL), lambda j: (nrm(j)[0], nrm(j)[1], 0)),
                pl.BlockSpec((1, HALO, D_MODEL), lambda j: (nrm(j)[0], jnp.maximum(nrm(j)[1] * hb - 1, 0), 0)),
                pl.BlockSpec((1, HALO, D_MODEL),
                             lambda j: (nrm(j)[0], jnp.minimum((nrm(j)[1] + 1) * hb, n_hb - 1), 0)),
                pl.BlockSpec((1, N_MOD, D_MODEL), lambda j: (nrm(j)[0] * (mod_row0 == 0) + mod_row0, 0, 0)),
                _const_spec((1, D_MODEL)),
                _const_spec((D_MODEL, D_IN)),
                _const_spec((LRU_CONV_W, D_RNN)),
                _const_spec((1, D_RNN))]
    args = [x, x, x, mod, nw, w_in, conv_w, conv_b]
    if rope:
        in_specs += [pl.BlockSpec((tm, HEAD_DIM), lambda j: (prj(j)[1], 0))] * 2
        args += list(tables)
    out_specs, out_shape = [], []
    for (_, width, kind) in plan:
        dt = F32 if kind == "conv" else BF16
        out_specs.append(pl.BlockSpec((1, tm, width), tok_out))
        out_shape.append(jax.ShapeDtypeStruct((bsz, length, width), dt))
    return pl.pallas_call(
        functools.partial(_inproj_kernel, plan=tuple(plan), rope=rope, tm=tm, nb=nb, n_steps=n_steps),
        grid=(n_steps,),
        in_specs=in_specs, out_specs=out_specs, out_shape=out_shape,
        scratch_shapes=[pltpu.VMEM((2, tm + 2 * HALO, D_MODEL), BF16),
                        pltpu.VMEM((D_RNN // LANES, 2 * (tm + 2 * HALO), LANES), F32)],
        compiler_params=pltpu.CompilerParams(dimension_semantics=("arbitrary",),
                                             vmem_limit_bytes=VMEM_LIMIT),
        name="inproj_x" if rope else "inproj_ctx",
    )(*args)


def _scan_kernel(xf_ref, xb_ref, xc_ref, wg_ref, bg_ref, lam_ref, hf_ref, hb_ref,
                 a3, b3, h3, hc_scr, *, tm, nb, clen, bsz):
    i = pl.program_id(0)
    chains = [(d, b) for d in range(N_DIRS) for b in range(bsz)]
    lg_per_tile = GATE_TILE // LANES

    def gate_piece(src_ref, ws, d, b, ch, kt, n):
        lam = lam_ref[d]
        softplus = jnp.maximum(-lam, 0.0) + jnp.log(1.0 + jnp.exp(-jnp.abs(lam)))
        hd2 = (-0.5 * LRU_C * LOG2E) * softplus
        cs = slice(kt * GATE_TILE, (kt + 1) * GATE_TILE)
        xk = src_ref[b, pl.ds(0, n), cs]
        g = jnp.dot(xk.astype(BF16), wg_ref[d, kt], preferred_element_type=F32)
        t_r = jnp.tanh(g[:, :GATE_TILE] + 0.5 * bg_ref[d, 0:1, cs])
        t_i = jnp.tanh(g[:, GATE_TILE:] + 0.5 * bg_ref[d, 1:2, cs])
        a = jnp.exp2(hd2[:, cs] * t_r + hd2[:, cs])
        y = 1.0 - a * a
        bb = (y * lax.rsqrt(jnp.maximum(y, TINY)) * xk) * (t_i + 1.0)
        for gg in range(lg_per_tile):
            lg = kt * lg_per_tile + gg
            ls = slice(gg * LANES, (gg + 1) * LANES)
            a3[ws, ch, pl.ds(0, n // SUBLANES), pl.ds(lg, SUBLANES, stride=LANE_GROUPS), :] = (
                a[:, ls].reshape(n // SUBLANES, SUBLANES, LANES))
            b3[ws, ch, pl.ds(0, n // SUBLANES), pl.ds(lg, SUBLANES, stride=LANE_GROUPS), :] = (
                bb[:, ls].reshape(n // SUBLANES, SUBLANES, LANES))

    def scan_group(rs, n8, tau, hs, keep):
        for rho in range(SUBLANES):
            for ch, (d, b) in enumerate(chains):
                tt, rr = (tau, rho) if d == 0 else (n8 - 1 - tau, SUBLANES - 1 - rho)
                win = pl.ds(rr * LANE_GROUPS, LANE_GROUPS)
                h = a3[rs, ch, tt, win, :] * hs[ch] + b3[rs, ch, tt, win, :]
                hs[ch] = h
                if keep:
                    h3[ch, tt, win, :] = h

    def emit(rows):
        r0, r1 = rows
        for ch, (d, b) in enumerate(chains):
            lo, hi = (r0, r1) if d == 0 else (tm - r1, tm - r0)
            o_ref = hf_ref if d == 0 else hb_ref
            for lg in range(LANE_GROUPS):
                slab = h3[ch, pl.ds(lo // SUBLANES, (hi - lo) // SUBLANES), pl.ds(lg, SUBLANES, stride=LANE_GROUPS), :]
                o_ref[b, pl.ds(lo, hi - lo), lg * LANES:(lg + 1) * LANES] = (
                    slab.reshape(hi - lo, LANES).astype(o_ref.dtype))

    @pl.when(i == 0)
    def _():
        for ch, (d, b) in enumerate(chains):
            for kt in range(GATE_TILES):
                gate_piece(xc_ref, 1, d, b, ch, kt, clen)

        def body(tau, hs):
            hs = list(hs)
            scan_group(1, clen // SUBLANES, tau, hs, False)
            return tuple(hs)

        hs = lax.fori_loop(0, clen // SUBLANES, body, (jnp.zeros((LANE_GROUPS, LANES), F32),) * len(chains))
        for ch in range(len(chains)):
            hc_scr[ch] = hs[ch]

    def step(ws, rs, gate, scan):
        n8 = tm // SUBLANES
        pieces = [(d, b, ch, kt) for ch, (d, b) in enumerate(chains) for kt in range(GATE_TILES)]
        per_piece = n8 // len(pieces)
        hs = [hc_scr[ch] for ch in range(len(chains))] if scan else None
        tau = 0
        for (d, b, ch, kt) in pieces:
            if gate:
                gate_piece(xf_ref if d == 0 else xb_ref, ws, d, b, ch, kt, tm)
            for _ in range(per_piece if scan else 0):
                scan_group(rs, n8, tau, hs, True)
                tau += 1
                if tau == n8 // 2:
                    emit((0, tm // 2))
        if scan:
            emit((tm // 2, tm))
            for ch in range(len(chains)):
                hc_scr[ch] = hs[ch]

    pl.when(i == 0)(functools.partial(step, 0, 1, True, False))
    for parity in range(2):
        pl.when((i > 0) & (i < nb) & (i % 2 == parity))(functools.partial(step, parity, 1 - parity, True, True))
    pl.when(i == nb)(functools.partial(step, nb % 2, 1 - nb % 2, False, True))


def _scan_call(xc, xc_ctx, wg, bg, lam, tm):
    bsz, length, ch = xc.shape
    clen = xc_ctx.shape[1]
    nb = length // tm
    n_chains = N_DIRS * bsz
    assert clen <= tm and (tm // SUBLANES) % (n_chains * GATE_TILES) == 0
    rows8 = tm // SUBLANES
    slab = (n_chains, rows8, LANE_GROUPS * SUBLANES, LANES)
    gi = lambda i: jnp.minimum(i, nb - 1)
    si = lambda i: jnp.maximum(i - 1, 0)
    return pl.pallas_call(
        functools.partial(_scan_kernel, tm=tm, nb=nb, clen=clen, bsz=bsz),
        grid=(nb + 1,),
        in_specs=[pl.BlockSpec((bsz, tm, ch), lambda i: (0, gi(i), 0)),
                  pl.BlockSpec((bsz, tm, ch), lambda i: (0, nb - 1 - gi(i), 0)),
                  _const_spec((bsz, clen, ch)),
                  _const_spec((N_DIRS, GATE_TILES, GATE_TILE, 2 * GATE_TILE)),
                  _const_spec((N_DIRS, 2, ch)),
                  _const_spec((N_DIRS, 1, ch))],
        out_specs=[pl.BlockSpec((bsz, tm, ch), lambda i: (0, si(i), 0)),
                   pl.BlockSpec((bsz, tm, ch), lambda i: (0, nb - 1 - si(i), 0))],
        out_shape=[jax.ShapeDtypeStruct((bsz, length, ch), BF16)] * 2,
        scratch_shapes=[pltpu.VMEM((2,) + slab, F32), pltpu.VMEM((2,) + slab, F32), pltpu.VMEM(slab, F32),
                        pltpu.VMEM((n_chains, LANE_GROUPS, LANES), F32)],
        compiler_params=pltpu.CompilerParams(dimension_semantics=("arbitrary",),
                                             vmem_limit_bytes=VMEM_LIMIT),
        name="scan",
    )(xc, xc, xc_ctx, wg, bg, lam)


def _attention(i, sink_ref, q_ref, k_ref, kp_ref, kn_ref, v_ref, vp_ref, vn_ref, kc_ref, vc_ref,
               ya_scr, s_scr, kcat, vcat, vctx, *, nqb, length):
    tq = nqb * BLOCK
    clen = kc_ref.shape[1]
    kcat[pl.ds(0, BLOCK), :] = kp_ref[0]
    kcat[pl.ds(BLOCK, tq), :] = k_ref[0]
    kcat[pl.ds(BLOCK + tq, BLOCK), :] = kn_ref[0]
    for g in range(N_KV_HEADS):
        gs = slice(g * HEAD_DIM, (g + 1) * HEAD_DIM)
        vs = slice(2 * g * HEAD_DIM, (2 * g + 1) * HEAD_DIM)
        os_ = slice((2 * g + 1) * HEAD_DIM, (2 * g + 2) * HEAD_DIM)
        vcat[pl.ds(0, BLOCK), vs] = vp_ref[0, :, gs]
        vcat[pl.ds(BLOCK, tq), vs] = v_ref[0, :, gs]
        vcat[pl.ds(BLOCK + tq, BLOCK), vs] = vn_ref[0, :, gs]
        vcat[:, os_] = jnp.ones((tq + 2 * BLOCK, HEAD_DIM), BF16)
        vctx[:, vs] = vc_ref[0, :, gs]
        vctx[:, os_] = jnp.ones((clen, HEAD_DIM), BF16)

    qi = lax.broadcasted_iota(jnp.int32, (BLOCK, 3 * BLOCK), 0)
    kj = lax.broadcasted_iota(jnp.int32, (BLOCK, 3 * BLOCK), 1)
    rel = kj - qi
    band = (rel >= BLOCK - WINDOW) & (rel <= BLOCK + WINDOW)
    nt = (((1,), (1,)), ((), ()))
    rows = Q_PER_KV * BLOCK
    masks = []
    for jq in range(nqb):
        kpos = (i * nqb + jq - 1) * BLOCK + kj
        masks.append((band & (kpos >= 0) & (kpos < length))[None])

    n_loc = 3 * BLOCK

    def scores(jq, g, buf):
        gs = slice(g * HEAD_DIM, (g + 1) * HEAD_DIM)
        q4 = jnp.concatenate(
            [q_ref[0, pl.ds(jq * BLOCK, BLOCK), h * HEAD_DIM:(h + 1) * HEAD_DIM]
             for h in range(g * Q_PER_KV, (g + 1) * Q_PER_KV)], axis=0)
        s_scr[buf, :, 0:n_loc] = lax.dot_general(q4, kcat[pl.ds(jq * BLOCK, n_loc), gs], nt,
                                                 preferred_element_type=F32)
        s_scr[buf, :, n_loc:n_loc + clen] = lax.dot_general(q4, kc_ref[0, :, gs], nt, preferred_element_type=F32)

    def softmax_values(jq, g, buf):
        vos = slice(2 * g * HEAD_DIM, (2 * g + 2) * HEAD_DIM)
        heads = [g * Q_PER_KV + hh for hh in range(Q_PER_KV)]
        sink = jnp.concatenate([jnp.full((BLOCK, 1), sink_ref[h] * LOG2E, F32) for h in heads], axis=0)
        cols = []
        for c in range(n_loc // BLOCK):
            s = s_scr[buf, :, c * BLOCK:(c + 1) * BLOCK]
            if c != 1:
                s = jnp.where(masks[jq][:, :, c * BLOCK:(c + 1) * BLOCK], s.reshape(Q_PER_KV, BLOCK, BLOCK),
                              NEG_INF).reshape(rows, BLOCK)
            cols.append(s)
        cols += [s_scr[buf, :, n_loc + c * LANES:n_loc + (c + 1) * LANES] for c in range(clen // LANES)]
        m = jnp.maximum(jnp.max(functools.reduce(jnp.maximum, cols), axis=-1, keepdims=True), sink)
        p = [jnp.exp2(s - m).astype(BF16) for s in cols]
        p_loc = jnp.concatenate(p[:n_loc // BLOCK], axis=1)
        p_ctx = jnp.concatenate(p[n_loc // BLOCK:], axis=1)
        o2 = (jnp.dot(p_loc, vcat[pl.ds(jq * BLOCK, n_loc), vos], preferred_element_type=F32)
              + jnp.dot(p_ctx, vctx[:, vos], preferred_element_type=F32))
        denom = o2[:, HEAD_DIM:] + jnp.exp2(sink - m)
        o = o2[:, :HEAD_DIM] * (1.0 / denom)
        for hh, h in enumerate(heads):
            ya_scr[pl.ds(jq * BLOCK, BLOCK), h * HEAD_DIM:(h + 1) * HEAD_DIM] = (
                o[hh * BLOCK:(hh + 1) * BLOCK].astype(ya_scr.dtype))

    order = [(jq, g) for jq in range(nqb) for g in range(N_KV_HEADS)]
    return [(functools.partial(scores, jq, g, u % 2), functools.partial(softmax_values, jq, g, u % 2))
            for u, (jq, g) in enumerate(order)]


def _mix_kernel(*refs, nqb, length, n_steps):
    (sink_ref, q_ref, k_ref, kp_ref, kn_ref, v_ref, vp_ref, vn_ref, kc_ref, vc_ref,
     hf_ref, hb_ref, rg_ref, ga_ref, gb_ref, x_ref, mod_ref, wr_ref, wa_ref, wo_ref, nw_ref,
     x1_ref, h2_ref, ya_scr, yr_scr, s_scr, kcat, vcat, vctx) = refs
    j = pl.program_id(0)
    slot = j % 2
    nb = length // (nqb * BLOCK)

    @pl.when(j == 0)
    def _():
        ya_scr[1] = jnp.zeros(ya_scr.shape[1:], ya_scr.dtype)

    units = _attention(jnp.minimum(j, n_steps - 2) % nb, sink_ref, q_ref, k_ref, kp_ref, kn_ref, v_ref, vp_ref,
                       vn_ref, kc_ref, vc_ref, ya_scr.at[slot], s_scr, kcat, vcat, vctx, nqb=nqb, length=length)

    state = {}

    def rnn_in():
        y_rnn = (hf_ref[0].astype(F32) + hb_ref[0].astype(F32)) * rg_ref[0].astype(F32)
        yr_scr[...] = y_rnn.astype(BF16)

    def branch_piece(p, w_ref, src, key):
        cs = slice(p * MXU_DIM, (p + 1) * MXU_DIM)
        state[key, p] = jnp.dot(src(), w_ref[:, cs], preferred_element_type=F32)

    def out_piece(p):
        cs = slice(p * MXU_DIM, (p + 1) * MXU_DIM)
        y = (ga_ref[0, :, cs].astype(F32) * state.pop(("r", p))
             + gb_ref[0, :, cs].astype(F32) * state.pop(("a", p)))
        part = jnp.dot(y.astype(BF16), wo_ref[cs, :], preferred_element_type=F32)
        state["z"] = part if "z" not in state else state["z"] + part

    n_p = D_MODEL // MXU_DIM
    att = [functools.partial(branch_piece, p, wa_ref, lambda: ya_scr[1 - slot], "a") for p in range(n_p)]
    rnn = [functools.partial(branch_piece, p, wr_ref, lambda: yr_scr[...], "r") for p in range(n_p)]
    outs = [functools.partial(out_piece, p) for p in range(n_p)]
    fillers = [rnn_in] + att + rnn[:n_p - 1] + [outs[0], rnn[n_p - 1]] + outs[1:]
    per_unit = MIX_FILLERS_PER_UNIT
    assert sum(per_unit) == len(fillers) and len(per_unit) == len(units)
    units[0][0]()
    for u, (_, softmax_values) in enumerate(units):
        if u + 1 < len(units):
            units[u + 1][0]()
        for f in fillers[:per_unit[u]]:
            f()
        fillers = fillers[per_unit[u]:]
        softmax_values()

    x1 = x_ref[0] + mod_ref[0, 2:3, :] * state["z"]
    x1_ref[0] = x1
    ms = jnp.mean(x1 * x1, axis=-1, keepdims=True)
    n2 = x1 * lax.rsqrt(ms + EPS) * nw_ref[...]
    h2_ref[0] = (n2 * (1.0 + mod_ref[0, 4:5, :]) + mod_ref[0, 3:4, :]).astype(h2_ref.dtype)


def _mix_call(sink, q, k, v, kc, vc, hf, hb, rg, ga, gb, x, mod, w_o_rnn, w_o_attn, w_out, nw2, nqb):
    bsz, length, _ = x.shape
    clen = kc.shape[1]
    tm = nqb * BLOCK
    n_blk = length // BLOCK
    nb = length // tm
    n_steps = bsz * nb + 1
    att = lambda j: divmod(jnp.minimum(j, n_steps - 2), nb)
    mrg = lambda j: divmod(jnp.maximum(j - 1, 0), nb)
    main = lambda j: (att(j)[0], att(j)[1], 0)
    prev = lambda j: (att(j)[0], jnp.maximum(att(j)[1] * nqb - 1, 0), 0)
    nxt = lambda j: (att(j)[0], jnp.minimum((att(j)[1] + 1) * nqb, n_blk - 1), 0)
    kv_specs = [pl.BlockSpec((1, tm, KV_WIDTH), main),
                pl.BlockSpec((1, BLOCK, KV_WIDTH), prev),
                pl.BlockSpec((1, BLOCK, KV_WIDTH), nxt)]
    ctx_spec = pl.BlockSpec((1, clen, KV_WIDTH), lambda j: (att(j)[0], 0, 0))
    blk = pl.BlockSpec((1, tm, D_MODEL), lambda j: (mrg(j)[0], mrg(j)[1], 0))
    return pl.pallas_call(
        functools.partial(_mix_kernel, nqb=nqb, length=length, n_steps=n_steps),
        grid=(n_steps,),
        in_specs=[pl.BlockSpec(memory_space=pltpu.SMEM),
                  pl.BlockSpec((1, tm, ATTN_WIDTH), main)] + kv_specs + kv_specs + [ctx_spec, ctx_spec]
                 + [blk, blk, blk, blk, blk, blk,
                    pl.BlockSpec((1, N_MOD, D_MODEL), lambda j: (mrg(j)[0], 0, 0)),
                    _const_spec((D_RNN, D_MODEL)), _const_spec((ATTN_WIDTH, D_MODEL)),
                    _const_spec((D_MODEL, D_MODEL)), _const_spec((1, D_MODEL))],
        out_specs=[blk, blk],
        out_shape=[jax.ShapeDtypeStruct((bsz, length, D_MODEL), F32),
                   jax.ShapeDtypeStruct((bsz, length, D_MODEL), BF16)],
        scratch_shapes=[pltpu.VMEM((2, tm, ATTN_WIDTH), BF16),
                        pltpu.VMEM((tm, D_RNN), BF16),
                        pltpu.VMEM((2, Q_PER_KV * BLOCK, 3 * BLOCK + clen), F32),
                        pltpu.VMEM((tm + 2 * BLOCK, KV_WIDTH), BF16),
                        pltpu.VMEM((tm + 2 * BLOCK, 2 * KV_WIDTH), BF16),
                        pltpu.VMEM((clen, 2 * KV_WIDTH), BF16)],
        compiler_params=pltpu.CompilerParams(dimension_semantics=("arbitrary",),
                                             vmem_limit_bytes=VMEM_LIMIT),
        name="mix",
    )(sink, q, k, k, k, v, v, v, kc, vc, hf, hb, rg, ga, gb, x, mod, w_o_rnn, w_o_attn, w_out, nw2)


def _ffn_kernel(h_ref, hp_ref, hn_ref, x1_ref, mod_ref, wu_ref, cw_ref, cb_ref, wd_ref, fw_ref,
                o_ref, hbuf, ua_scr, uv_scr, g_scr, *, tm, nb, n_steps):
    j = pl.program_id(0)
    i = jnp.minimum(j, n_steps - 2) % nb
    ring = j % 2

    def stage_rows():
        hbuf[pl.ds(0, HALO), :] = jnp.where(i > 0, hp_ref[0], jnp.zeros_like(hp_ref[0]))
        hbuf[pl.ds(HALO, tm), :] = h_ref[0]
        hbuf[pl.ds(HALO + tm, HALO), :] = jnp.where(i < nb - 1, hn_ref[0], jnp.zeros_like(hn_ref[0]))

    rows = tm + 2 * HALO

    n_chunks = D_FF // FFN_CHUNK
    pieces = FFN_CHUNK // MXU_DIM
    lg_per_piece = MXU_DIM // LANES

    def up_piece(u_scr, c, p, c0):
        u = jnp.dot(hbuf[...], wu_ref[:, c0 + p * MXU_DIM:c0 + (p + 1) * MXU_DIM], preferred_element_type=F32)
        for q in range(lg_per_piece):
            u_scr[c % 2, p * lg_per_piece + q, pl.ds(0, rows, stride=2), :] = u[:, q * LANES:(q + 1) * LANES]

    def conv(u_scr, slot, lg, c0):
        cs = slice(c0 + lg * LANES, c0 + (lg + 1) * LANES)
        taps = [u_scr[slot, lg, pl.ds(2 * (HALO - FFN_CONV_W // 2 + k), tm, stride=2), :] for k in range(FFN_CONV_W)]
        return cb_ref[:, cs] + sum(cw_ref[k:k + 1, cs] * taps[k] for k in range(FFN_CONV_W))

    def act_piece(c, lg):
        ca, cv = c * FFN_CHUNK, D_FF + c * FFN_CHUNK
        g = _gelu(conv(ua_scr, c % 2, lg, ca)) * conv(uv_scr, c % 2, lg, cv)
        g_scr[ring, :, ca + lg * LANES:ca + (lg + 1) * LANES] = g.astype(BF16)

    half = D_MODEL // 2

    z = [None, None]

    def down_piece(c, hh):
        ks = slice(c * FFN_CHUNK, (c + 1) * FFN_CHUNK)
        part = jnp.dot(g_scr[1 - ring, :, ks], wd_ref[ks, hh * half:(hh + 1) * half].astype(BF16),
                       preferred_element_type=F32)
        z[hh] = part if z[hh] is None else z[hh] + part

    @pl.when(j == 0)
    def _():
        g_scr[1] = jnp.zeros(g_scr.shape[1:], g_scr.dtype)

    downs = [functools.partial(down_piece, c, hh) for c in range(n_chunks) for hh in range(2)]
    n_down = list(FFN_DOWNS_PER_CHUNK[:-1]) + [len(downs)]
    stage_rows()
    for p in range(pieces):
        up_piece(ua_scr, 0, p, 0)
        up_piece(uv_scr, 0, p, D_FF)
    for c in range(n_chunks):
        mm = []
        if c + 1 < n_chunks:
            for p in range(pieces):
                mm.append(functools.partial(up_piece, ua_scr, c + 1, p, (c + 1) * FFN_CHUNK))
                mm.append(functools.partial(up_piece, uv_scr, c + 1, p, D_FF + (c + 1) * FFN_CHUNK))
        mm += downs[:n_down[c]]
        downs = downs[n_down[c]:]
        acts = [functools.partial(act_piece, c, lg) for lg in range(FFN_CHUNK // LANES)]
        while mm or acts:
            if mm:
                mm.pop(0)()
            if acts:
                acts.pop(0)()
    x2 = x1_ref[0] + mod_ref[0, 5:6, :] * jnp.concatenate(z, axis=1)
    ms = jnp.mean(x2 * x2, axis=-1, keepdims=True)
    o_ref[0] = x2 * lax.rsqrt(ms + EPS) * fw_ref[...]


def _ffn_call(h2, x1, mod, w_up, conv_w, conv_b, w_down, fw, tm):
    bsz, length, _ = x1.shape
    nb = length // tm
    n_steps = bsz * nb + 1
    hb = tm // HALO
    n_hb = length // HALO
    act = lambda j: divmod(jnp.minimum(j, n_steps - 2), nb)
    dwn = lambda j: divmod(jnp.maximum(j - 1, 0), nb)
    tok_out = lambda j: (dwn(j)[0], dwn(j)[1], 0)
    return pl.pallas_call(
        functools.partial(_ffn_kernel, tm=tm, nb=nb, n_steps=n_steps),
        grid=(n_steps,),
        in_specs=[pl.BlockSpec((1, tm, D_MODEL), lambda j: (act(j)[0], act(j)[1], 0)),
                  pl.BlockSpec((1, HALO, D_MODEL), lambda j: (act(j)[0], jnp.maximum(act(j)[1] * hb - 1, 0), 0)),
                  pl.BlockSpec((1, HALO, D_MODEL),
                               lambda j: (act(j)[0], jnp.minimum((act(j)[1] + 1) * hb, n_hb - 1), 0)),
                  pl.BlockSpec((1, tm, D_MODEL), tok_out),
                  pl.BlockSpec((1, N_MOD, D_MODEL), lambda j: (dwn(j)[0], 0, 0)),
                  _const_spec((D_MODEL, 2 * D_FF)), _const_spec((FFN_CONV_W, 2 * D_FF)),
                  _const_spec((1, 2 * D_FF)), _const_spec((D_FF, D_MODEL)), _const_spec((1, D_MODEL))],
        out_specs=pl.BlockSpec((1, tm, D_MODEL), tok_out),
        out_shape=jax.ShapeDtypeStruct((bsz, length, D_MODEL), F32),
        scratch_shapes=[pltpu.VMEM((tm + 2 * HALO, D_MODEL), BF16),
                        pltpu.VMEM((2, FFN_CHUNK // LANES, 2 * (tm + 2 * HALO), LANES), F32),
                        pltpu.VMEM((2, FFN_CHUNK // LANES, 2 * (tm + 2 * HALO), LANES), F32),
                        pltpu.VMEM((2, tm, D_FF), BF16)],
        compiler_params=pltpu.CompilerParams(dimension_semantics=("arbitrary",),
                                             vmem_limit_bytes=VMEM_LIMIT),
        name="ffn",
    )(h2, h2, h2, x1, mod, w_up, conv_w, conv_b, w_down, fw)


def _rope_tables(length):
    pos = np.arange(length)
    row, col = pos // GRID_W, pos % GRID_W
    half = HEAD_DIM // 2
    inv = ROPE_THETA ** (-np.arange(0, half, 2, dtype=np.float64) / half)
    ar, ac = row[:, None] * inv, col[:, None] * inv
    cos = np.concatenate([np.cos(ar), np.cos(ar), np.cos(ac), np.cos(ac)], axis=1)
    sin = np.concatenate([-np.sin(ar), np.sin(ar), -np.sin(ac), np.sin(ac)], axis=1)
    return jnp.asarray(cos, F32), jnp.asarray(sin, F32)


def _gate_weights(w_a, w_x):
    per = GATE_TILE // LRU_BLOCK
    eye = jnp.eye(per, dtype=F32)

    def bd(w):
        w = w.reshape(N_DIRS, GATE_TILES, per, LRU_BLOCK, LRU_BLOCK)
        return jnp.einsum('dtpcn,pq->dtpcqn', w, eye).reshape(N_DIRS, GATE_TILES, GATE_TILE, GATE_TILE)

    return jnp.concatenate([bd(w_a), bd(w_x)], axis=-1).astype(BF16)


def kernel(x, c, ctx, c_ctx, w_mod, b_mod, norm1_w, w_in, lru_conv_w, lru_conv_b, gate_a_w, gate_a_b,
           gate_x_w, gate_x_b, lru_lambda, sink_logit, w_o_rnn, w_o_attn, w_out, norm2_w, w_up,
           ffn_conv_w, ffn_conv_b, w_down, final_norm_w):
    bsz, length, _ = x.shape
    l = 0

    cs = jnp.concatenate([c, c_ctx[None], jnp.zeros((SUBLANES - bsz - 1, D_MODEL), F32)], axis=0)
    mod = _mod_call(cs, w_mod[l], b_mod[l][None]).reshape(SUBLANES, N_MOD, D_MODEL)

    w_in_b = w_in[l]
    nw1 = norm1_w[l][None]
    cw, cb = lru_conv_w[l], lru_conv_b[l][None]

    plan_x = [(COL_R, D_RNN, "conv"), (COL_RG, D_RNN, "gelu"), (COL_Q, ATTN_WIDTH, "rope_q"),
              (COL_K, KV_WIDTH, "rope_k"), (COL_V, KV_WIDTH, "bf16"), (COL_GA, D_MODEL, "sigmoid"),
              (COL_GB, D_MODEL, "sigmoid")]
    xc, rg, q, k, v, ga, gb = _inproj_call(x, mod, 0, nw1, w_in_b, cw, cb, _rope_tables(length), plan_x, tm=512)
    plan_c = [(COL_R, D_RNN, "conv"), (COL_K, KV_WIDTH, "bf16"), (COL_V, KV_WIDTH, "bf16")]
    xc_ctx, kc, vc = _inproj_call(ctx, mod, bsz, nw1, w_in_b, cw, cb, None, plan_c, tm=ctx.shape[1])

    wg = _gate_weights(gate_a_w[l], gate_x_w[l])
    bg = jnp.stack([gate_a_b[l], gate_x_b[l]], axis=1)
    hf, hb = _scan_call(xc, xc_ctx, wg, bg, lru_lambda[l][:, None, :], tm=256)

    x1, h2 = _mix_call(sink_logit[l], q, k, v, kc, vc, hf, hb, rg, ga, gb, x, mod, w_o_rnn[l].astype(BF16),
                       w_o_attn[l].astype(BF16), w_out[l].astype(BF16), norm2_w[l][None], nqb=4)
    return _ffn_call(h2, x1, mod, w_up[l].astype(BF16), ffn_conv_w[l], ffn_conv_b[l][None],
                     w_down[l], final_norm_w[None], tm=512)
```

```python
import functools

import numpy as np
import jax
import jax.numpy as jnp
from jax import lax
from jax.experimental import pallas as pl
from jax.experimental.pallas import tpu as pltpu

F32 = jnp.float32
BF16 = jnp.bfloat16

D_MODEL = 1024
N_HEADS = 8
N_KV_HEADS = 2
HEAD_DIM = 128
Q_PER_KV = N_HEADS // N_KV_HEADS
ATTN_WIDTH = N_HEADS * HEAD_DIM
KV_WIDTH = N_KV_HEADS * HEAD_DIM
WINDOW = 128
BLOCK = 128
GRID_W = 64
ROPE_THETA = 10000.0
D_RNN = D_MODEL
N_LRU_BLOCKS = 16
LRU_BLOCK = D_RNN // N_LRU_BLOCKS
LRU_C = 8.0
LRU_CONV_W = 4
LRU_CONV_LEFT = 2
N_DIRS = 2
D_FF = 3 * D_MODEL
FFN_CONV_W = 3
N_MOD = 6
EPS = 1e-6
NEG_INF = -1e30

COL_R = 0
COL_RG = COL_R + D_RNN
COL_Q = COL_RG + D_RNN
COL_K = COL_Q + ATTN_WIDTH
COL_V = COL_K + KV_WIDTH
COL_GA = COL_V + KV_WIDTH
COL_GB = COL_GA + D_MODEL
D_IN = COL_GB + D_MODEL

SUBLANES = 8
BF16_SUBLANES = 16
LANES = 128
MXU_DIM = 256
VMEM_LIMIT = 56 * 1024 * 1024

LANE_GROUPS = D_RNN // LANES
GATE_TILE = MXU_DIM
GATE_TILES = D_RNN // GATE_TILE
FFN_CHUNK = 512
FFN_DOWNS_PER_CHUNK = (0, 0, 1, 1, 2, 8)
MIX_FILLERS_PER_UNIT = (2, 2, 1, 2, 1, 2, 1, 2)
HALO = BF16_SUBLANES
LOG2E = float(np.log2(np.e))
assert BLOCK == LANES and WINDOW >= BLOCK - 1
TINY = 1e-30


GELU_K1 = float(np.sqrt(2.0 / np.pi))
GELU_K2 = GELU_K1 * 0.044715


def _gelu(v):
    return (0.5 * v) * (1.0 + jnp.tanh(v * (GELU_K1 + GELU_K2 * (v * v))))


def _sigmoid(v):
    return 0.5 * jnp.tanh(0.5 * v) + 0.5


def _const_spec(shape):
    nd = len(shape)
    return pl.BlockSpec(shape, lambda *_: (0,) * nd, pipeline_mode=pl.Buffered(1))


def _mod_kernel(c_ref, w_ref, b_ref, o_ref):
    c = c_ref[...]
    s = c * _sigmoid(c)
    o_ref[...] = jnp.dot(s, w_ref[...], preferred_element_type=F32) + b_ref[...]


def _mod_call(cs, w_mod, b_mod):
    n_out = w_mod.shape[1]
    tn = n_out // 4
    return pl.pallas_call(
        _mod_kernel,
        grid=(n_out // tn,),
        in_specs=[pl.BlockSpec((SUBLANES, D_MODEL), lambda j: (0, 0)),
                  pl.BlockSpec((D_MODEL, tn), lambda j: (0, j)),
                  pl.BlockSpec((1, tn), lambda j: (0, j))],
        out_specs=pl.BlockSpec((SUBLANES, tn), lambda j: (0, j)),
        out_shape=jax.ShapeDtypeStruct((SUBLANES, n_out), F32),
        compiler_params=pltpu.CompilerParams(dimension_semantics=("arbitrary",),
                                             vmem_limit_bytes=VMEM_LIMIT),
        name="mod",
    )(cs, w_mod, b_mod)


def _inproj_kernel(*refs, plan, rope, tm, nb, n_steps):
    x_ref, xp_ref, xn_ref, mod_ref, nw_ref, w_ref, cw_ref, cb_ref = refs[:8]
    n_in = 10 if rope else 8
    out_refs = refs[n_in:n_in + len(plan)]
    h_scr, r_scr = refs[-2:]
    j = pl.program_id(0)
    i = jnp.minimum(j, n_steps - 2) % nb
    slot = j % 2
    h_in = h_scr.at[1 - slot]

    def norm_mod(x):
        ms = jnp.mean(x * x, axis=-1, keepdims=True)
        y = x * lax.rsqrt(ms + EPS) * nw_ref[...]
        return y * (1.0 + mod_ref[0, 1:2, :]) + mod_ref[0, 0:1, :]

    def norm_stage():
        h_scr[slot, pl.ds(0, HALO), :] = jnp.where(i > 0, norm_mod(xp_ref[0]), 0.0).astype(BF16)
        h_scr[slot, pl.ds(HALO, tm), :] = norm_mod(x_ref[0]).astype(BF16)
        h_scr[slot, pl.ds(HALO + tm, HALO), :] = jnp.where(i < nb - 1, norm_mod(xn_ref[0]), 0.0).astype(BF16)

    def piece(c0, kind, o_ref, p):
        cs = slice(p * MXU_DIM, (p + 1) * MXU_DIM)
        wc = slice(c0 + p * MXU_DIM, c0 + (p + 1) * MXU_DIM)
        if kind == "conv":
            r = jnp.dot(h_in[...], w_ref[:, wc], preferred_element_type=F32)
            for q in range(MXU_DIM // LANES):
                lg = p * (MXU_DIM // LANES) + q
                ls = slice(lg * LANES, (lg + 1) * LANES)
                r_scr[lg, pl.ds(0, tm + 2 * HALO, stride=2), :] = r[:, q * LANES:(q + 1) * LANES]
                taps = [r_scr[lg, pl.ds(2 * (HALO - LRU_CONV_LEFT + k), tm, stride=2), :] for k in range(LRU_CONV_W)]
                o_ref[0, :, ls] = 0.5 * cb_ref[:, ls] + sum((0.5 * cw_ref[k:k + 1, ls]) * taps[k]
                                                            for k in range(LRU_CONV_W))
            return
        res = jnp.dot(h_in[pl.ds(HALO, tm), :], w_ref[:, wc], preferred_element_type=F32)
        if kind in ("rope_q", "rope_k"):
            scale = HEAD_DIM ** -0.5 * LOG2E if kind == "rope_q" else None
            cos, sin = refs[8][...], refs[9][...]
            lane = lax.broadcasted_iota(jnp.int32, (tm, HEAD_DIM), 1)
            low = (lane % (HEAD_DIM // 2)) < (HEAD_DIM // 4)
            for hh in range(MXU_DIM // HEAD_DIM):
                v = res[:, hh * HEAD_DIM:(hh + 1) * HEAD_DIM]
                partner = jnp.where(low, pltpu.roll(v, HEAD_DIM - HEAD_DIM // 4, 1), pltpu.roll(v, HEAD_DIM // 4, 1))
                rot = v * cos + partner * sin
                if scale is not None:
                    rot = rot * scale
                o_ref[0, :, p * MXU_DIM + hh * HEAD_DIM:p * MXU_DIM + (hh + 1) * HEAD_DIM] = rot.astype(o_ref.dtype)
        else:
            act = {"gelu": _gelu, "sigmoid": _sigmoid, "bf16": lambda t: t}[kind]
            o_ref[0, :, cs] = act(res).astype(o_ref.dtype)

    def project():
        by_kind = {}
        for (c0, width, kind), o_ref in zip(plan, out_refs):
            by_kind.setdefault(kind, []).extend(
                functools.partial(piece, c0, kind, o_ref, p) for p in range(width // MXU_DIM))
        take = lambda kind: by_kind[kind].pop(0)() if by_kind.get(kind) else None
        take("conv"), take("sigmoid")
        norm_stage()
        take("sigmoid"), take("bf16")
        while by_kind.get("conv"):
            take("conv"), take("sigmoid"), take("sigmoid"), take("bf16")
        while by_kind.get("rope_q") or by_kind.get("rope_k"):
            take("rope_q"), take("gelu"), take("rope_k"), take("bf16")
        for kind in list(by_kind):
            while by_kind[kind]:
                take(kind)

    pl.when(j == 0)(norm_stage)
    pl.when(j > 0)(project)


def _inproj_call(x, mod, mod_row0, nw, w_in, conv_w, conv_b, tables, plan, tm):
    bsz, length, _ = x.shape
    rope = tables is not None
    nb = length // tm
    n_steps = bsz * nb + 1
    hb = tm // HALO
    n_hb = length // HALO
    nrm = lambda j: divmod(jnp.minimum(j, n_steps - 2), nb)
    prj = lambda j: divmod(jnp.maximum(j - 1, 0), nb)
    tok_out = lambda j: (prj(j)[0], prj(j)[1], 0)
    in_specs = [pl.BlockSpec((1, tm, D_MODEL), lambda j: (nrm(j)[0], nrm(j)[1], 0)),
                pl.BlockSpec((1, HALO, D_MODEL), lambda j: (nrm(j)[0], jnp.maximum(nrm(j)[1] * hb - 1, 0), 0)),
                pl.BlockSpec((1, HALO, D_MODEL),
                             lambda j: (nrm(j)[0], jnp.minimum((nrm(j)[1] + 1) * hb, n_hb - 1), 0)),
                pl.BlockSpec((1, N_MOD, D_MODEL), lambda j: (nrm(j)[0] * (mod_row0 == 0) + mod_row0, 0, 0)),
                _const_spec((1, D_MODEL)),
                _const_spec((D_MODEL, D_IN)),
                _const_spec((LRU_CONV_W, D_RNN)),
                _const_spec((1, D_RNN))]
    args = [x, x, x, mod, nw, w_in, conv_w, conv_b]
    if rope:
        in_specs += [pl.BlockSpec((tm, HEAD_DIM), lambda j: (prj(j)[1], 0))] * 2
        args += list(tables)
    out_specs, out_shape = [], []
    for (_, width, kind) in plan:
        dt = F32 if kind == "conv" else BF16
        out_specs.append(pl.BlockSpec((1, tm, width), tok_out))
        out_shape.append(jax.ShapeDtypeStruct((bsz, length, width), dt))
    return pl.pallas_call(
        functools.partial(_inproj_kernel, plan=tuple(plan), rope=rope, tm=tm, nb=nb, n_steps=n_steps),
        grid=(n_steps,),
        in_specs=in_specs, out_specs=out_specs, out_shape=out_shape,
        scratch_shapes=[pltpu.VMEM((2, tm + 2 * HALO, D_MODEL), BF16),
                        pltpu.VMEM((D_RNN // LANES, 2 * (tm + 2 * HALO), LANES), F32)],
        compiler_params=pltpu.CompilerParams(dimension_semantics=("arbitrary",),
                                             vmem_limit_bytes=VMEM_LIMIT),
        name="inproj_x" if rope else "inproj_ctx",
    )(*args)


def _scan_kernel(xf_ref, xb_ref, xc_ref, wg_ref, bg_ref, lam_ref, hf_ref, hb_ref,
                 a3, b3, h3, hc_scr, *, tm, nb, clen, bsz):
    i = pl.program_id(0)
    chains = [(d, b) for d in range(N_DIRS) for b in range(bsz)]
    lg_per_tile = GATE_TILE // LANES

    def gate_piece(src_ref, ws, d, b, ch, kt, n):
        lam = lam_ref[d]
        softplus = jnp.maximum(-lam, 0.0) + jnp.log(1.0 + jnp.exp(-jnp.abs(lam)))
        hd2 = (-0.5 * LRU_C * LOG2E) * softplus
        cs = slice(kt * GATE_TILE, (kt + 1) * GATE_TILE)
        xk = src_ref[b, pl.ds(0, n), cs]
        g = jnp.dot(xk.astype(BF16), wg_ref[d, kt], preferred_element_type=F32)
        t_r = jnp.tanh(g[:, :GATE_TILE] + 0.5 * bg_ref[d, 0:1, cs])
        t_i = jnp.tanh(g[:, GATE_TILE:] + 0.5 * bg_ref[d, 1:2, cs])
        a = jnp.exp2(hd2[:, cs] * t_r + hd2[:, cs])
        y = 1.0 - a * a
        bb = (y * lax.rsqrt(jnp.maximum(y, TINY)) * xk) * (t_i + 1.0)
        for gg in range(lg_per_tile):
            lg = kt * lg_per_tile + gg
            ls = slice(gg * LANES, (gg + 1) * LANES)
            a3[ws, ch, pl.ds(0, n // SUBLANES), pl.ds(lg, SUBLANES, stride=LANE_GROUPS), :] = (
                a[:, ls].reshape(n // SUBLANES, SUBLANES, LANES))
            b3[ws, ch, pl.ds(0, n // SUBLANES), pl.ds(lg, SUBLANES, stride=LANE_GROUPS), :] = (
                bb[:, ls].reshape(n // SUBLANES, SUBLANES, LANES))

    def scan_group(rs, n8, tau, hs, keep):
        for rho in range(SUBLANES):
            for ch, (d, b) in enumerate(chains):
                tt, rr = (tau, rho) if d == 0 else (n8 - 1 - tau, SUBLANES - 1 - rho)
                win = pl.ds(rr * LANE_GROUPS, LANE_GROUPS)
                h = a3[rs, ch, tt, win, :] * hs[ch] + b3[rs, ch, tt, win, :]
                hs[ch] = h
                if keep:
                    h3[ch, tt, win, :] = h

    def emit(rows):
        r0, r1 = rows
        for ch, (d, b) in enumerate(chains):
            lo, hi = (r0, r1) if d == 0 else (tm - r1, tm - r0)
            o_ref = hf_ref if d == 0 else hb_ref
            for lg in range(LANE_GROUPS):
                slab = h3[ch, pl.ds(lo // SUBLANES, (hi - lo) // SUBLANES), pl.ds(lg, SUBLANES, stride=LANE_GROUPS), :]
                o_ref[b, pl.ds(lo, hi - lo), lg * LANES:(lg + 1) * LANES] = (
                    slab.reshape(hi - lo, LANES).astype(o_ref.dtype))

    @pl.when(i == 0)
    def _():
        for ch, (d, b) in enumerate(chains):
            for kt in range(GATE_TILES):
                gate_piece(xc_ref, 1, d, b, ch, kt, clen)

        def body(tau, hs):
            hs = list(hs)
            scan_group(1, clen // SUBLANES, tau, hs, False)
            return tuple(hs)

        hs = lax.fori_loop(0, clen // SUBLANES, body, (jnp.zeros((LANE_GROUPS, LANES), F32),) * len(chains))
        for ch in range(len(chains)):
            hc_scr[ch] = hs[ch]

    def step(ws, rs, gate, scan):
        n8 = tm // SUBLANES
        pieces = [(d, b, ch, kt) for ch, (d, b) in enumerate(chains) for kt in range(GATE_TILES)]
        per_piece = n8 // len(pieces)
        hs = [hc_scr[ch] for ch in range(len(chains))] if scan else None
        tau = 0
        for (d, b, ch, kt) in pieces:
            if gate:
                gate_piece(xf_ref if d == 0 else xb_ref, ws, d, b, ch, kt, tm)
            for _ in range(per_piece if scan else 0):
                scan_group(rs, n8, tau, hs, True)
                tau += 1
                if tau == n8 // 2:
                    emit((0, tm // 2))
        if scan:
            emit((tm // 2, tm))
            for ch in range(len(chains)):
                hc_scr[ch] = hs[ch]

    pl.when(i == 0)(functools.partial(step, 0, 1, True, False))
    for parity in range(2):
        pl.when((i > 0) & (i < nb) & (i % 2 == parity))(functools.partial(step, parity, 1 - parity, True, True))
    pl.when(i == nb)(functools.partial(step, nb % 2, 1 - nb % 2, False, True))


def _scan_call(xc, xc_ctx, wg, bg, lam, tm):
    bsz, length, ch = xc.shape
    clen = xc_ctx.shape[1]
    nb = length // tm
    n_chains = N_DIRS * bsz
    assert clen <= tm and (tm // SUBLANES) % (n_chains * GATE_TILES) == 0
    rows8 = tm // SUBLANES
    slab = (n_chains, rows8, LANE_GROUPS * SUBLANES, LANES)
    gi = lambda i: jnp.minimum(i, nb - 1)
    si = lambda i: jnp.maximum(i - 1, 0)
    return pl.pallas_call(
        functools.partial(_scan_kernel, tm=tm, nb=nb, clen=clen, bsz=bsz),
        grid=(nb + 1,),
        in_specs=[pl.BlockSpec((bsz, tm, ch), lambda i: (0, gi(i), 0)),
                  pl.BlockSpec((bsz, tm, ch), lambda i: (0, nb - 1 - gi(i), 0)),
                  _const_spec((bsz, clen, ch)),
                  _const_spec((N_DIRS, GATE_TILES, GATE_TILE, 2 * GATE_TILE)),
                  _const_spec((N_DIRS, 2, ch)),
                  _const_spec((N_DIRS, 1, ch))],
        out_specs=[pl.BlockSpec((bsz, tm, ch), lambda i: (0, si(i), 0)),
                   pl.BlockSpec((bsz, tm, ch), lambda i: (0, nb - 1 - si(i), 0))],
        out_shape=[jax.ShapeDtypeStruct((bsz, length, ch), BF16)] * 2,
        scratch_shapes=[pltpu.VMEM((2,) + slab, F32), pltpu.VMEM((2,) + slab, F32), pltpu.VMEM(slab, F32),
                        pltpu.VMEM((n_chains, LANE_GROUPS, LANES), F32)],
        compiler_params=pltpu.CompilerParams(dimension_semantics=("arbitrary",),
                                             vmem_limit_bytes=VMEM_LIMIT),
        name="scan",
    )(xc, xc, xc_ctx, wg, bg, lam)


def _attention(i, sink_ref, q_ref, k_ref, kp_ref, kn_ref, v_ref, vp_ref, vn_ref, kc_ref, vc_ref,
               ya_scr, s_scr, kcat, vcat, vctx, *, nqb, length):
    tq = nqb * BLOCK
    clen = kc_ref.shape[1]
    kcat[pl.ds(0, BLOCK), :] = kp_ref[0]
    kcat[pl.ds(BLOCK, tq), :] = k_ref[0]
    kcat[pl.ds(BLOCK + tq, BLOCK), :] = kn_ref[0]
    for g in range(N_KV_HEADS):
        gs = slice(g * HEAD_DIM, (g + 1) * HEAD_DIM)
        vs = slice(2 * g * HEAD_DIM, (2 * g + 1) * HEAD_DIM)
        os_ = slice((2 * g + 1) * HEAD_DIM, (2 * g + 2) * HEAD_DIM)
        vcat[pl.ds(0, BLOCK), vs] = vp_ref[0, :, gs]
        vcat[pl.ds(BLOCK, tq), vs] = v_ref[0, :, gs]
        vcat[pl.ds(BLOCK + tq, BLOCK), vs] = vn_ref[0, :, gs]
        vcat[:, os_] = jnp.ones((tq + 2 * BLOCK, HEAD_DIM), BF16)
        vctx[:, vs] = vc_ref[0, :, gs]
        vctx[:, os_] = jnp.ones((clen, HEAD_DIM), BF16)

    qi = lax.broadcasted_iota(jnp.int32, (BLOCK, 3 * BLOCK), 0)
    kj = lax.broadcasted_iota(jnp.int32, (BLOCK, 3 * BLOCK), 1)
    rel = kj - qi
    band = (rel >= BLOCK - WINDOW) & (rel <= BLOCK + WINDOW)
    nt = (((1,), (1,)), ((), ()))
    rows = Q_PER_KV * BLOCK
    masks = []
    for jq in range(nqb):
        kpos = (i * nqb + jq - 1) * BLOCK + kj
        masks.append((band & (kpos >= 0) & (kpos < length))[None])

    n_loc = 3 * BLOCK

    def scores(jq, g, buf):
        gs = slice(g * HEAD_DIM, (g + 1) * HEAD_DIM)
        q4 = jnp.concatenate(
            [q_ref[0, pl.ds(jq * BLOCK, BLOCK), h * HEAD_DIM:(h + 1) * HEAD_DIM]
             for h in range(g * Q_PER_KV, (g + 1) * Q_PER_KV)], axis=0)
        s_scr[buf, :, 0:n_loc] = lax.dot_general(q4, kcat[pl.ds(jq * BLOCK, n_loc), gs], nt,
                                                 preferred_element_type=F32)
        s_scr[buf, :, n_loc:n_loc + clen] = lax.dot_general(q4, kc_ref[0, :, gs], nt, preferred_element_type=F32)

    def softmax_values(jq, g, buf):
        vos = slice(2 * g * HEAD_DIM, (2 * g + 2) * HEAD_DIM)
        heads = [g * Q_PER_KV + hh for hh in range(Q_PER_KV)]
        sink = jnp.concatenate([jnp.full((BLOCK, 1), sink_ref[h] * LOG2E, F32) for h in heads], axis=0)
        cols = []
        for c in range(n_loc // BLOCK):
            s = s_scr[buf, :, c * BLOCK:(c + 1) * BLOCK]
            if c != 1:
                s = jnp.where(masks[jq][:, :, c * BLOCK:(c + 1) * BLOCK], s.reshape(Q_PER_KV, BLOCK, BLOCK),
                              NEG_INF).reshape(rows, BLOCK)
            cols.append(s)
        cols += [s_scr[buf, :, n_loc + c * LANES:n_loc + (c + 1) * LANES] for c in range(clen // LANES)]
        m = jnp.maximum(jnp.max(functools.reduce(jnp.maximum, cols), axis=-1, keepdims=True), sink)
        p = [jnp.exp2(s - m).astype(BF16) for s in cols]
        p_loc = jnp.concatenate(p[:n_loc // BLOCK], axis=1)
        p_ctx = jnp.concatenate(p[n_loc // BLOCK:], axis=1)
        o2 = (jnp.dot(p_loc, vcat[pl.ds(jq * BLOCK, n_loc), vos], preferred_element_type=F32)
              + jnp.dot(p_ctx, vctx[:, vos], preferred_element_type=F32))
        denom = o2[:, HEAD_DIM:] + jnp.exp2(sink - m)
        o = o2[:, :HEAD_DIM] * (1.0 / denom)
        for hh, h in enumerate(heads):
            ya_scr[pl.ds(jq * BLOCK, BLOCK), h * HEAD_DIM:(h + 1) * HEAD_DIM] = (
                o[hh * BLOCK:(hh + 1) * BLOCK].astype(ya_scr.dtype))

    order = [(jq, g) for jq in range(nqb) for g in range(N_KV_HEADS)]
    return [(functools.partial(scores, jq, g, u % 2), functools.partial(softmax_values, jq, g, u % 2))
            for u, (jq, g) in enumerate(order)]


def _mix_kernel(*refs, nqb, length, n_steps):
    (sink_ref, q_ref, k_ref, kp_ref, kn_ref, v_ref, vp_ref, vn_ref, kc_ref, vc_ref,
     hf_ref, hb_ref, rg_ref, ga_ref, gb_ref, x_ref, mod_ref, wr_ref, wa_ref, wo_ref, nw_ref,
     x1_ref, h2_ref, ya_scr, yr_scr, s_scr, kcat, vcat, vctx) = refs
    j = pl.program_id(0)
    slot = j % 2
    nb = length // (nqb * BLOCK)

    @pl.when(j == 0)
    def _():
        ya_scr[1] = jnp.zeros(ya_scr.shape[1:], ya_scr.dtype)

    units = _attention(jnp.minimum(j, n_steps - 2) % nb, sink_ref, q_ref, k_ref, kp_ref, kn_ref, v_ref, vp_ref,
                       vn_ref, kc_ref, vc_ref, ya_scr.at[slot], s_scr, kcat, vcat, vctx, nqb=nqb, length=length)

    state = {}

    def rnn_in():
        y_rnn = (hf_ref[0].astype(F32) + hb_ref[0].astype(F32)) * rg_ref[0].astype(F32)
        yr_scr[...] = y_rnn.astype(BF16)

    def branch_piece(p, w_ref, src, key):
        cs = slice(p * MXU_DIM, (p + 1) * MXU_DIM)
        state[key, p] = jnp.dot(src(), w_ref[:, cs], preferred_element_type=F32)

    def out_piece(p):
        cs = slice(p * MXU_DIM, (p + 1) * MXU_DIM)
        y = (ga_ref[0, :, cs].astype(F32) * state.pop(("r", p))
             + gb_ref[0, :, cs].astype(F32) * state.pop(("a", p)))
        part = jnp.dot(y.astype(BF16), wo_ref[cs, :], preferred_element_type=F32)
        state["z"] = part if "z" not in state else state["z"] + part

    n_p = D_MODEL // MXU_DIM
    att = [functools.partial(branch_piece, p, wa_ref, lambda: ya_scr[1 - slot], "a") for p in range(n_p)]
    rnn = [functools.partial(branch_piece, p, wr_ref, lambda: yr_scr[...], "r") for p in range(n_p)]
    outs = [functools.partial(out_piece, p) for p in range(n_p)]
    fillers = [rnn_in] + att + rnn[:n_p - 1] + [outs[0], rnn[n_p - 1]] + outs[1:]
    per_unit = MIX_FILLERS_PER_UNIT
    assert sum(per_unit) == len(fillers) and len(per_unit) == len(units)
    units[0][0]()
    for u, (_, softmax_values) in enumerate(units):
        if u + 1 < len(units):
            units[u + 1][0]()
        for f in fillers[:per_unit[u]]:
            f()
        fillers = fillers[per_unit[u]:]
        softmax_values()

    x1 = x_ref[0] + mod_ref[0, 2:3, :] * state["z"]
    x1_ref[0] = x1
    ms = jnp.mean(x1 * x1, axis=-1, keepdims=True)
    n2 = x1 * lax.rsqrt(ms + EPS) * nw_ref[...]
    h2_ref[0] = (n2 * (1.0 + mod_ref[0, 4:5, :]) + mod_ref[0, 3:4, :]).astype(h2_ref.dtype)


def _mix_call(sink, q, k, v, kc, vc, hf, hb, rg, ga, gb, x, mod, w_o_rnn, w_o_attn, w_out, nw2, nqb):
    bsz, length, _ = x.shape
    clen = kc.shape[1]
    tm = nqb * BLOCK
    n_blk = length // BLOCK
    nb = length // tm
    n_steps = bsz * nb + 1
    att = lambda j: divmod(jnp.minimum(j, n_steps - 2), nb)
    mrg = lambda j: divmod(jnp.maximum(j - 1, 0), nb)
    main = lambda j: (att(j)[0], att(j)[1], 0)
    prev = lambda j: (att(j)[0], jnp.maximum(att(j)[1] * nqb - 1, 0), 0)
    nxt = lambda j: (att(j)[0], jnp.minimum((att(j)[1] + 1) * nqb, n_blk - 1), 0)
    kv_specs = [pl.BlockSpec((1, tm, KV_WIDTH), main),
                pl.BlockSpec((1, BLOCK, KV_WIDTH), prev),
                pl.BlockSpec((1, BLOCK, KV_WIDTH), nxt)]
    ctx_spec = pl.BlockSpec((1, clen, KV_WIDTH), lambda j: (att(j)[0], 0, 0))
    blk = pl.BlockSpec((1, tm, D_MODEL), lambda j: (mrg(j)[0], mrg(j)[1], 0))
    return pl.pallas_call(
        functools.partial(_mix_kernel, nqb=nqb, length=length, n_steps=n_steps),
        grid=(n_steps,),
        in_specs=[pl.BlockSpec(memory_space=pltpu.SMEM),
                  pl.BlockSpec((1, tm, ATTN_WIDTH), main)] + kv_specs + kv_specs + [ctx_spec, ctx_spec]
                 + [blk, blk, blk, blk, blk, blk,
                    pl.BlockSpec((1, N_MOD, D_MODEL), lambda j: (mrg(j)[0], 0, 0)),
                    _const_spec((D_RNN, D_MODEL)), _const_spec((ATTN_WIDTH, D_MODEL)),
                    _const_spec((D_MODEL, D_MODEL)), _const_spec((1, D_MODEL))],
        out_specs=[blk, blk],
        out_shape=[jax.ShapeDtypeStruct((bsz, length, D_MODEL), F32),
                   jax.ShapeDtypeStruct((bsz, length, D_MODEL), BF16)],
        scratch_shapes=[pltpu.VMEM((2, tm, ATTN_WIDTH), BF16),
                        pltpu.VMEM((tm, D_RNN), BF16),
                        pltpu.VMEM((2, Q_PER_KV * BLOCK, 3 * BLOCK + clen), F32),
                        pltpu.VMEM((tm + 2 * BLOCK, KV_WIDTH), BF16),
                        pltpu.VMEM((tm + 2 * BLOCK, 2 * KV_WIDTH), BF16),
                        pltpu.VMEM((clen, 2 * KV_WIDTH), BF16)],
        compiler_params=pltpu.CompilerParams(dimension_semantics=("arbitrary",),
                                             vmem_limit_bytes=VMEM_LIMIT),
        name="mix",
    )(sink, q, k, k, k, v, v, v, kc, vc, hf, hb, rg, ga, gb, x, mod, w_o_rnn, w_o_attn, w_out, nw2)


def _ffn_kernel(h_ref, hp_ref, hn_ref, x1_ref, mod_ref, wu_ref, cw_ref, cb_ref, wd_ref, fw_ref,
                o_ref, hbuf, ua_scr, uv_scr, g_scr, *, tm, nb, n_steps):
    j = pl.program_id(0)
    i = jnp.minimum(j, n_steps - 2) % nb
    ring = j % 2

    def stage_rows():
        hbuf[pl.ds(0, HALO), :] = jnp.where(i > 0, hp_ref[0], jnp.zeros_like(hp_ref[0]))
        hbuf[pl.ds(HALO, tm), :] = h_ref[0]
        hbuf[pl.ds(HALO + tm, HALO), :] = jnp.where(i < nb - 1, hn_ref[0], jnp.zeros_like(hn_ref[0]))

    rows = tm + 2 * HALO

    n_chunks = D_FF // FFN_CHUNK
    pieces = FFN_CHUNK // MXU_DIM
    lg_per_piece = MXU_DIM // LANES

    def up_piece(u_scr, c, p, c0):
        u = jnp.dot(hbuf[...], wu_ref[:, c0 + p * MXU_DIM:c0 + (p + 1) * MXU_DIM], preferred_element_type=F32)
        for q in range(lg_per_piece):
            u_scr[c % 2, p * lg_per_piece + q, pl.ds(0, rows, stride=2), :] = u[:, q * LANES:(q + 1) * LANES]

    def conv(u_scr, slot, lg, c0):
        cs = slice(c0 + lg * LANES, c0 + (lg + 1) * LANES)
        taps = [u_scr[slot, lg, pl.ds(2 * (HALO - FFN_CONV_W // 2 + k), tm, stride=2), :] for k in range(FFN_CONV_W)]
        return cb_ref[:, cs] + sum(cw_ref[k:k + 1, cs] * taps[k] for k in range(FFN_CONV_W))

    def act_piece(c, lg):
        ca, cv = c * FFN_CHUNK, D_FF + c * FFN_CHUNK
        g = _gelu(conv(ua_scr, c % 2, lg, ca)) * conv(uv_scr, c % 2, lg, cv)
        g_scr[ring, :, ca + lg * LANES:ca + (lg + 1) * LANES] = g.astype(BF16)

    half = D_MODEL // 2

    z = [None, None]

    def down_piece(c, hh):
        ks = slice(c * FFN_CHUNK, (c + 1) * FFN_CHUNK)
        part = jnp.dot(g_scr[1 - ring, :, ks], wd_ref[ks, hh * half:(hh + 1) * half].astype(BF16),
                       preferred_element_type=F32)
        z[hh] = part if z[hh] is None else z[hh] + part

    @pl.when(j == 0)
    def _():
        g_scr[1] = jnp.zeros(g_scr.shape[1:], g_scr.dtype)

    downs = [functools.partial(down_piece, c, hh) for c in range(n_chunks) for hh in range(2)]
    n_down = list(FFN_DOWNS_PER_CHUNK[:-1]) + [len(downs)]
    stage_rows()
    for p in range(pieces):
        up_piece(ua_scr, 0, p, 0)
        up_piece(uv_scr, 0, p, D_FF)
    for c in range(n_chunks):
        mm = []
        if c + 1 < n_chunks:
            for p in range(pieces):
                mm.append(functools.partial(up_piece, ua_scr, c + 1, p, (c + 1) * FFN_CHUNK))
                mm.append(functools.partial(up_piece, uv_scr, c + 1, p, D_FF + (c + 1) * FFN_CHUNK))
        mm += downs[:n_down[c]]
        downs = downs[n_down[c]:]
        acts = [functools.partial(act_piece, c, lg) for lg in range(FFN_CHUNK // LANES)]
        while mm or acts:
            if mm:
                mm.pop(0)()
            if acts:
                acts.pop(0)()
    x2 = x1_ref[0] + mod_ref[0, 5:6, :] * jnp.concatenate(z, axis=1)
    ms = jnp.mean(x2 * x2, axis=-1, keepdims=True)
    o_ref[0] = x2 * lax.rsqrt(ms + EPS) * fw_ref[...]


def _ffn_call(h2, x1, mod, w_up, conv_w, conv_b, w_down, fw, tm):
    bsz, length, _ = x1.shape
    nb = length // tm
    n_steps = bsz * nb + 1
    hb = tm // HALO
    n_hb = length // HALO
    act = lambda j: divmod(jnp.minimum(j, n_steps - 2), nb)
    dwn = lambda j: divmod(jnp.maximum(j - 1, 0), nb)
    tok_out = lambda j: (dwn(j)[0], dwn(j)[1], 0)
    return pl.pallas_call(
        functools.partial(_ffn_kernel, tm=tm, nb=nb, n_steps=n_steps),
        grid=(n_steps,),
        in_specs=[pl.BlockSpec((1, tm, D_MODEL), lambda j: (act(j)[0], act(j)[1], 0)),
                  pl.BlockSpec((1, HALO, D_MODEL), lambda j: (act(j)[0], jnp.maximum(act(j)[1] * hb - 1, 0), 0)),
                  pl.BlockSpec((1, HALO, D_MODEL),
                               lambda j: (act(j)[0], jnp.minimum((act(j)[1] + 1) * hb, n_hb - 1), 0)),
                  pl.BlockSpec((1, tm, D_MODEL), tok_out),
                  pl.BlockSpec((1, N_MOD, D_MODEL), lambda j: (dwn(j)[0], 0, 0)),
                  _const_spec((D_MODEL, 2 * D_FF)), _const_spec((FFN_CONV_W, 2 * D_FF)),
                  _const_spec((1, 2 * D_FF)), _const_spec((D_FF, D_MODEL)), _const_spec((1, D_MODEL))],
        out_specs=pl.BlockSpec((1, tm, D_MODEL), tok_out),
        out_shape=jax.ShapeDtypeStruct((bsz, length, D_MODEL), F32),
        scratch_shapes=[pltpu.VMEM((tm + 2 * HALO, D_MODEL), BF16),
                        pltpu.VMEM((2, FFN_CHUNK // LANES, 2 * (tm + 2 * HALO), LANES), F32),
                        pltpu.VMEM((2, FFN_CHUNK // LANES, 2 * (tm + 2 * HALO), LANES), F32),
                        pltpu.VMEM((2, tm, D_FF), BF16)],
        compiler_params=pltpu.CompilerParams(dimension_semantics=("arbitrary",),
                                             vmem_limit_bytes=VMEM_LIMIT),
        name="ffn",
    )(h2, h2, h2, x1, mod, w_up, conv_w, conv_b, w_down, fw)


def _rope_tables(length):
    pos = np.arange(length)
    row, col = pos // GRID_W, pos % GRID_W
    half = HEAD_DIM // 2
    inv = ROPE_THETA ** (-np.arange(0, half, 2, dtype=np.float64) / half)
    ar, ac = row[:, None] * inv, col[:, None] * inv
    cos = np.concatenate([np.cos(ar), np.cos(ar), np.cos(ac), np.cos(ac)], axis=1)
    sin = np.concatenate([-np.sin(ar), np.sin(ar), -np.sin(ac), np.sin(ac)], axis=1)
    return jnp.asarray(cos, F32), jnp.asarray(sin, F32)


def _gate_weights(w_a, w_x):
    per = GATE_TILE // LRU_BLOCK
    eye = jnp.eye(per, dtype=F32)

    def bd(w):
        w = w.reshape(N_DIRS, GATE_TILES, per, LRU_BLOCK, LRU_BLOCK)
        return jnp.einsum('dtpcn,pq->dtpcqn', w, eye).reshape(N_DIRS, GATE_TILES, GATE_TILE, GATE_TILE)

    return jnp.concatenate([bd(w_a), bd(w_x)], axis=-1).astype(BF16)


def kernel(x, c, ctx, c_ctx, w_mod, b_mod, norm1_w, w_in, lru_conv_w, lru_conv_b, gate_a_w, gate_a_b,
           gate_x_w, gate_x_b, lru_lambda, sink_logit, w_o_rnn, w_o_attn, w_out, norm2_w, w_up,
           ffn_conv_w, ffn_conv_b, w_down, final_norm_w):
    bsz, length, _ = x.shape
    l = 0

    cs = jnp.concatenate([c, c_ctx[None], jnp.zeros((SUBLANES - bsz - 1, D_MODEL), F32)], axis=0)
    mod = _mod_call(cs, w_mod[l], b_mod[l][None]).reshape(SUBLANES, N_MOD, D_MODEL)

    w_in_b = w_in[l].astype(BF16)
    nw1 = norm1_w[l][None]
    cw, cb = lru_conv_w[l], lru_conv_b[l][None]

    plan_x = [(COL_R, D_RNN, "conv"), (COL_RG, D_RNN, "gelu"), (COL_Q, ATTN_WIDTH, "rope_q"),
              (COL_K, KV_WIDTH, "rope_k"), (COL_V, KV_WIDTH, "bf16"), (COL_GA, D_MODEL, "sigmoid"),
              (COL_GB, D_MODEL, "sigmoid")]
    xc, rg, q, k, v, ga, gb = _inproj_call(x, mod, 0, nw1, w_in_b, cw, cb, _rope_tables(length), plan_x, tm=512)
    plan_c = [(COL_R, D_RNN, "conv"), (COL_K, KV_WIDTH, "bf16"), (COL_V, KV_WIDTH, "bf16")]
    xc_ctx, kc, vc = _inproj_call(ctx, mod, bsz, nw1, w_in_b, cw, cb, None, plan_c, tm=ctx.shape[1])

    wg = _gate_weights(gate_a_w[l], gate_x_w[l])
    bg = jnp.stack([gate_a_b[l], gate_x_b[l]], axis=1)
    hf, hb = _scan_call(xc, xc_ctx, wg, bg, lru_lambda[l][:, None, :], tm=256)

    x1, h2 = _mix_call(sink_logit[l], q, k, v, kc, vc, hf, hb, rg, ga, gb, x, mod, w_o_rnn[l].astype(BF16),
                       w_o_attn[l].astype(BF16), w_out[l].astype(BF16), norm2_w[l][None], nqb=4)
    return _ffn_call(h2, x1, mod, w_up[l].astype(BF16), ffn_conv_w[l], ffn_conv_b[l][None],
                     w_down[l], final_norm_w[None], tm=512)
```
